```python
import math
import jax
import jax.numpy as jnp
from jax import lax
import numpy as np

D_MODEL = 4096
BATCH = 2
SEQ = 4096
DEPTH = 1

RMS_EPS = 1e-5

RWKV_HEAD = 64
RWKV_WIDTH = D_MODEL
RWKV_HEADS = RWKV_WIDTH // RWKV_HEAD
W_LORA = max(32, int(round(1.8 * D_MODEL ** 0.5 / 32)) * 32)
A_LORA = max(32, int(round(1.8 * D_MODEL ** 0.5 / 32)) * 32)
G_LORA = max(32, int(round(0.6 * D_MODEL ** 0.8 / 32)) * 32)
RWKV_LN_EPS = 64e-5
RWKV_COLS = 3 * RWKV_WIDTH + W_LORA + A_LORA + G_LORA

SSM_INNER = 2 * D_MODEL
SSM_HEADDIM = 64
SSM_HEADS = SSM_INNER // SSM_HEADDIM
SSM_GROUPS = 8
SSM_STATE = 128
SSM_CONV = 4
SSM_CHUNK = 128
SSM_XBC = SSM_INNER + 2 * SSM_GROUPS * SSM_STATE

MLP_HIDDEN = 4 * D_MODEL

IN_COLS = RWKV_COLS + SSM_INNER + SSM_XBC + SSM_HEADS + 2 * D_MODEL

kernel_name = "rwkv7_mamba2_gated_hybrid_block"


def _split(x, sizes):
    return jnp.split(x, [int(s) for s in np.cumsum(sizes)[:-1]], axis=-1)


def _rmsnorm(x, w, eps=RMS_EPS):
    xf = x.astype(jnp.float32)
    y = xf * lax.rsqrt(jnp.mean(xf * xf, axis=-1, keepdims=True) + eps)
    return (y * w.astype(jnp.float32)).astype(x.dtype)


def _token_shift(p):
    return jnp.pad(p[:, :-1], ((0, 0), (1, 0), (0, 0)))


def _rwkv7_recurrence(r, w, k, v, a, b):
    bsz, _, heads, n = r.shape

    def step(S, inp):
        r_t, w_t, k_t, v_t, a_t, b_t = inp
        sa = jnp.einsum('bhij,bhj->bhi', S, a_t)
        S = S * w_t[:, :, None, :] + sa[..., None] * b_t[:, :, None, :] + v_t[..., None] * k_t[:, :, None, :]
        return S, jnp.einsum('bhij,bhj->bhi', S, r_t)

    xs = tuple(jnp.moveaxis(t, 1, 0) for t in (r, w, k, v, a, b))
    s0 = jnp.zeros((bsz, heads, n, n), jnp.float32)
    _, y = lax.scan(step, s0, xs)
    return jnp.moveaxis(y, 0, 1)


def _heads(t, bsz, t_len):
    return t.reshape(bsz, t_len, RWKV_HEADS, RWKV_HEAD)


def _rwkv7_mixer(p, mu, w0, w_up, a0, a_up, g_up, k_k, k_a, r_k, ln_w, ln_b):
    f32 = jnp.float32
    bsz, t_len, _ = p.shape
    p = p + (_token_shift(p) - p) * mu
    r, k, v, wd, ad, gd = _split(p, (RWKV_WIDTH, RWKV_WIDTH, RWKV_WIDTH, W_LORA, A_LORA, G_LORA))
    w = -jax.nn.softplus(-(w0 + jnp.tanh(wd) @ w_up).astype(f32)) - 0.5
    decay = jnp.exp(-jnp.exp(w))
    a = jax.nn.sigmoid((a0 + ad @ a_up).astype(f32))
    g = jax.nn.sigmoid(gd) @ g_up
    kk = _heads((k * k_k).astype(f32), bsz, t_len)
    kk = kk / jnp.maximum(jnp.linalg.norm(kk, axis=-1, keepdims=True), 1e-12)
    k_mod = k.astype(f32) * (1.0 + (a - 1.0) * k_a.astype(f32))
    r_h = _heads(r.astype(f32), bsz, t_len)
    k_h = _heads(k_mod, bsz, t_len)
    v_h = _heads(v.astype(f32), bsz, t_len)
    a_h = _heads(a, bsz, t_len)
    d_h = _heads(decay, bsz, t_len)
    y = _rwkv7_recurrence(r_h, d_h, k_h, v_h, -kk, kk * a_h)
    mean = jnp.mean(y, axis=-1, keepdims=True)
    var = jnp.mean(jnp.square(y - mean), axis=-1, keepdims=True)
    y = ((y - mean) * lax.rsqrt(var + RWKV_LN_EPS)).reshape(bsz, t_len, RWKV_WIDTH)
    y = y * ln_w.astype(f32) + ln_b.astype(f32)
    bonus = jnp.sum(r_h * k_h * r_k.astype(f32), axis=-1, keepdims=True) * v_h
    y = (y + bonus.reshape(bsz, t_len, RWKV_WIDTH)) * g.astype(f32)
    return y.astype(p.dtype)


def _causal_depthwise_conv(x, w, b):
    k_len, ch = w.shape
    y = lax.conv_general_dilated(x, w[:, None, :], window_strides=(1,), padding=[(k_len - 1, 0)],
                                 dimension_numbers=('NWC', 'WIO', 'NWC'), feature_group_count=ch)
    return y + b


def _segsum(x):
    t = x.shape[-1]
    cs = jnp.cumsum(x, axis=-1)
    diff = cs[..., :, None] - cs[..., None, :]
    return jnp.where(jnp.tril(jnp.ones((t, t), dtype=bool)), diff, -jnp.inf)


def _ssd_chunked(X, A, Bm, Cm):
    bsz, t_len, heads, hd = X.shape
    groups, n = Bm.shape[2], Bm.shape[3]
    e = heads // groups
    nc = t_len // SSM_CHUNK
    X = X.reshape(bsz, nc, SSM_CHUNK, groups, e, hd)
    Bm = Bm.reshape(bsz, nc, SSM_CHUNK, groups, n)
    Cm = Cm.reshape(bsz, nc, SSM_CHUNK, groups, n)
    A = A.reshape(bsz, nc, SSM_CHUNK, groups, e).transpose(0, 3, 4, 1, 2)
    a_cs = jnp.cumsum(A, axis=-1)
    L = jnp.exp(_segsum(A))
    cb = jnp.einsum('bclgn,bcsgn->bgcls', Cm, Bm)
    y_diag = jnp.einsum('bgecls,bcsgep->bclgep', cb[:, :, None] * L, X)
    decay_states = jnp.exp(a_cs[..., -1:] - a_cs).transpose(0, 3, 4, 1, 2)
    states = jnp.einsum('bclgn,bclgep->bcgepn', Bm, X * decay_states[..., None])
    chunk_a = jnp.pad(a_cs[..., -1], ((0, 0), (0, 0), (0, 0), (1, 0)))
    decay_chunk = jnp.exp(_segsum(chunk_a))
    states = jnp.pad(states, ((0, 0), (1, 0), (0, 0), (0, 0), (0, 0), (0, 0)))
    states = jnp.einsum('bgezc,bcgepn->bzgepn', decay_chunk, states)[:, :-1]
    state_decay_out = jnp.exp(a_cs).transpose(0, 3, 4, 1, 2)
    y_off = jnp.einsum('bclgn,bcgepn->bclgep', Cm, states) * state_decay_out[..., None]
    return (y_diag + y_off).reshape(bsz, t_len, heads, hd)


def _mamba2_mixer(z, xbc, dt, conv_w, conv_b, dt_bias, a_log, d_skip, norm_w):
    f32 = jnp.float32
    bsz, t_len, _ = z.shape
    xbc = jax.nn.silu(_causal_depthwise_conv(xbc, conv_w, conv_b))
    xs, bm, cm = _split(xbc, (SSM_INNER, SSM_GROUPS * SSM_STATE, SSM_GROUPS * SSM_STATE))
    dt = jax.nn.softplus(dt.astype(f32) + dt_bias.astype(f32))
    a = -jnp.exp(a_log.astype(f32))
    xh = xs.astype(f32).reshape(bsz, t_len, SSM_HEADS, SSM_HEADDIM)
    y = _ssd_chunked(xh * dt[..., None], dt * a,
                     bm.astype(f32).reshape(bsz, t_len, SSM_GROUPS, SSM_STATE),
                     cm.astype(f32).reshape(bsz, t_len, SSM_GROUPS, SSM_STATE))
    y = y + xh * d_skip.astype(f32)[:, None]
    y = y.reshape(bsz, t_len, SSM_INNER) * jax.nn.silu(z.astype(f32))
    yg = y.reshape(bsz, t_len, SSM_GROUPS, SSM_INNER // SSM_GROUPS)
    yg = yg * lax.rsqrt(jnp.mean(yg * yg, axis=-1, keepdims=True) + RMS_EPS)
    return (yg.reshape(bsz, t_len, SSM_INNER) * norm_w.astype(f32)).astype(z.dtype)


def _normal(key, shape, scale):
    return scale * jax.random.normal(key, shape, jnp.float32)


def setup_inputs(seed: int = 0) -> dict:
    key = jax.random.key(seed)
    ks = jax.random.split(key, 27)
    f32 = jnp.float32
    L = DEPTH
    x = _normal(ks[0], (BATCH, SEQ, D_MODEL), 1.0)
    norm_mix_w = 1.0 + _normal(ks[1], (L, D_MODEL), 0.02)
    w_in = _normal(ks[2], (L, D_MODEL, IN_COLS), D_MODEL ** -0.5)
    rwkv_mu = jax.random.uniform(ks[3], (L, RWKV_COLS), f32)
    ramp = (jnp.arange(RWKV_WIDTH, dtype=f32) / (RWKV_WIDTH - 1)) ** 0.85
    rwkv_w0 = -7.0 + 6.0 * ramp + _normal(ks[4], (L, RWKV_WIDTH), 0.1)
    rwkv_w_up = _normal(ks[5], (L, W_LORA, RWKV_WIDTH), 0.5 * W_LORA ** -0.5)
    rwkv_a0 = _normal(ks[6], (L, RWKV_WIDTH), 0.1)
    rwkv_a_up = _normal(ks[7], (L, A_LORA, RWKV_WIDTH), A_LORA ** -0.5)
    rwkv_g_up = _normal(ks[8], (L, G_LORA, RWKV_WIDTH), G_LORA ** -0.5)
    rwkv_k_k = 0.85 + _normal(ks[9], (L, RWKV_WIDTH), 0.02)
    rwkv_k_a = 1.0 + _normal(ks[10], (L, RWKV_WIDTH), 0.02)
    rwkv_r_k = _normal(ks[11], (L, RWKV_HEADS, RWKV_HEAD), 0.1)
    rwkv_ln_w = 1.0 + _normal(ks[12], (L, RWKV_WIDTH), 0.02)
    rwkv_ln_b = _normal(ks[13], (L, RWKV_WIDTH), 0.02)
    ssm_conv_w = _normal(ks[14], (L, SSM_CONV, SSM_XBC), SSM_CONV ** -0.5)
    ssm_conv_b = _normal(ks[15], (L, SSM_XBC), 0.02)
    dt0 = jnp.exp(jax.random.uniform(ks[16], (L, SSM_HEADS), f32, math.log(1e-3), math.log(1e-1)))
    ssm_dt_bias = dt0 + jnp.log(-jnp.expm1(-dt0))
    ssm_a_log = jnp.log(jax.random.uniform(ks[17], (L, SSM_HEADS), f32, 1.0, 16.0))
    ssm_d = 1.0 + _normal(ks[18], (L, SSM_HEADS), 0.02)
    ssm_norm_w = 1.0 + _normal(ks[19], (L, SSM_INNER), 0.02)
    w_br_rwkv = _normal(ks[20], (L, RWKV_WIDTH, D_MODEL), RWKV_WIDTH ** -0.5)
    w_br_ssm = _normal(ks[21], (L, SSM_INNER, D_MODEL), SSM_INNER ** -0.5)
    w_out = _normal(ks[22], (L, D_MODEL, D_MODEL), D_MODEL ** -0.5)
    norm_mlp_w = 1.0 + _normal(ks[23], (L, D_MODEL), 0.02)
    w_mlp_up = _normal(ks[24], (L, D_MODEL, MLP_HIDDEN), D_MODEL ** -0.5)
    w_mlp_down = _normal(ks[25], (L, MLP_HIDDEN, D_MODEL), MLP_HIDDEN ** -0.5)
    norm_final_w = 1.0 + _normal(ks[26], (D_MODEL,), 0.02)
    return {"x": x, "norm_mix_w": norm_mix_w, "w_in": w_in, "rwkv_mu": rwkv_mu,
            "rwkv_w0": rwkv_w0, "rwkv_w_up": rwkv_w_up, "rwkv_a0": rwkv_a0,
            "rwkv_a_up": rwkv_a_up, "rwkv_g_up": rwkv_g_up, "rwkv_k_k": rwkv_k_k,
            "rwkv_k_a": rwkv_k_a, "rwkv_r_k": rwkv_r_k, "rwkv_ln_w": rwkv_ln_w,
            "rwkv_ln_b": rwkv_ln_b, "ssm_conv_w": ssm_conv_w, "ssm_conv_b": ssm_conv_b,
            "ssm_dt_bias": ssm_dt_bias, "ssm_a_log": ssm_a_log, "ssm_d": ssm_d,
            "ssm_norm_w": ssm_norm_w, "w_br_rwkv": w_br_rwkv, "w_br_ssm": w_br_ssm,
            "w_out": w_out, "norm_mlp_w": norm_mlp_w, "w_mlp_up": w_mlp_up,
            "w_mlp_down": w_mlp_down, "norm_final_w": norm_final_w}


def reference(x, norm_mix_w, w_in, rwkv_mu, rwkv_w0, rwkv_w_up, rwkv_a0, rwkv_a_up, rwkv_g_up,
              rwkv_k_k, rwkv_k_a, rwkv_r_k, rwkv_ln_w, rwkv_ln_b, ssm_conv_w, ssm_conv_b,
              ssm_dt_bias, ssm_a_log, ssm_d, ssm_norm_w, w_br_rwkv, w_br_ssm, w_out,
              norm_mlp_w, w_mlp_up, w_mlp_down, norm_final_w):
    h = x
    for layer in range(DEPTH):
        u = _rmsnorm(h, norm_mix_w[layer])
        proj = u @ w_in[layer]
        p_rwkv, z, xbc, dt, gate_a, gate_b = _split(
            proj, (RWKV_COLS, SSM_INNER, SSM_XBC, SSM_HEADS, D_MODEL, D_MODEL))
        y_a = _rwkv7_mixer(p_rwkv, rwkv_mu[layer], rwkv_w0[layer], rwkv_w_up[layer],
                           rwkv_a0[layer], rwkv_a_up[layer], rwkv_g_up[layer],
                           rwkv_k_k[layer], rwkv_k_a[layer], rwkv_r_k[layer],
                           rwkv_ln_w[layer], rwkv_ln_b[layer]) @ w_br_rwkv[layer]
        y_b = _mamba2_mixer(z, xbc, dt, ssm_conv_w[layer], ssm_conv_b[layer],
                            ssm_dt_bias[layer], ssm_a_log[layer], ssm_d[layer],
                            ssm_norm_w[layer]) @ w_br_ssm[layer]
        merged = jax.nn.sigmoid(gate_a) * y_a + jax.nn.sigmoid(gate_b) * y_b
        h = h + merged @ w_out[layer]
        u = _rmsnorm(h, norm_mlp_w[layer])
        h = h + jnp.square(jax.nn.relu(u @ w_mlp_up[layer])) @ w_mlp_down[layer]
    return _rmsnorm(h, norm_final_w)
```

```python
import functools

import jax
import jax.numpy as jnp
from jax import lax
from jax.experimental import pallas as pl
from jax.experimental.pallas import tpu as pltpu

F32 = jnp.float32
BF16 = jnp.bfloat16
HIGHEST = lax.Precision.HIGHEST

LANES = 128
SUBLANES = 8
VMEM_LIMIT_BYTES = 56 * 1024 * 1024

RMS_EPS = 1e-5
RWKV_LN_EPS = 64e-5
RWKV_CHUNK = 64
RWKV_SOLVE_BLOCK = 16
SSM_GROUPS = 8
SSM_STATE = 128
SSM_CHUNK = 128


def _round_up(n, m):
    return (n + m - 1) // m * m


def _pick(dim, prefs):
    for p in prefs:
        if dim % p == 0:
            return p
    return dim


def _pad_to(a, axis, size):
    if a.shape[axis] == size:
        return a
    pad = [(0, 0)] * a.ndim
    pad[axis] = (0, size - a.shape[axis])
    return jnp.pad(a, pad)


def _params(sem):
    return pltpu.CompilerParams(dimension_semantics=sem, vmem_limit_bytes=VMEM_LIMIT_BYTES)


def _dot(a, b, precision=None):
    return jnp.dot(a, b, preferred_element_type=F32, precision=precision)


def _dot_nt(a, b, precision=None):
    return lax.dot_general(a, b, (((1,), (1,)), ((), ())), preferred_element_type=F32,
                           precision=precision)


def _dot_tn(a, b, precision=None):
    return lax.dot_general(a, b, (((0,), (0,)), ((), ())), preferred_element_type=F32,
                           precision=precision)


def _sigmoid(x):
    return 1.0 / (1.0 + jnp.exp(-x))


def _silu(x):
    return x * _sigmoid(x)


def _softplus(x):
    return jnp.maximum(x, 0.0) + jnp.log(1.0 + jnp.exp(-jnp.abs(x)))


def _rmsnorm_kernel(x_ref, w_ref, o_ref):
    x = x_ref[...]
    y = x * lax.rsqrt(jnp.mean(x * x, axis=-1, keepdims=True) + RMS_EPS)
    o_ref[...] = (y * w_ref[...]).astype(o_ref.dtype)


def _rmsnorm(x2d, w, out_dtype):
    m, d = x2d.shape
    tr = _pick(m, (256, 128, 64, 32, 16, 8))
    return pl.pallas_call(
        _rmsnorm_kernel,
        grid=(m // tr,),
        in_specs=[pl.BlockSpec((tr, d), lambda i: (i, 0)),
                  pl.BlockSpec((1, d), lambda i: (0, 0))],
        out_specs=pl.BlockSpec((tr, d), lambda i: (i, 0)),
        out_shape=jax.ShapeDtypeStruct((m, d), out_dtype),
        compiler_params=_params(("parallel",)),
        name="rmsnorm",
    )(x2d, w.reshape(1, d).astype(F32))


def _mm_kernel(*refs, n_extra, epilogue, nk):
    a_ref, w_ref = refs[0], refs[1]
    extras = refs[2:2 + n_extra]
    o_ref = refs[2 + n_extra]
    part = _dot(a_ref[...], w_ref[...])
    if nk == 1:
        o_ref[...] = epilogue(part, *[e[...] for e in extras]).astype(o_ref.dtype)
        return
    acc_ref = refs[3 + n_extra]
    k = pl.program_id(2)

    @pl.when(k == 0)
    def _():
        acc_ref[...] = part

    @pl.when(k > 0)
    def _():
        acc_ref[...] += part

    @pl.when(k == nk - 1)
    def _():
        o_ref[...] = epilogue(acc_ref[...], *[e[...] for e in extras]).astype(o_ref.dtype)


def _matmul(a, w, out_dtype, epilogue=None, extras=(), tk_max=2048, name="matmul"):
    m, k = a.shape
    _, n = w.shape
    if epilogue is None:
        epilogue = lambda acc: acc
    tm = _pick(m, (1024, 512, 256, 128, 64, 32, 16, 8))
    tn = _pick(n, (1024, 512, 256, 128))
    tk = k if k <= tk_max else _pick(k, (tk_max, 1024, 512, 256, 128))
    nk = k // tk
    in_specs = [pl.BlockSpec((tm, tk), lambda i, j, kk: (i, kk)),
                pl.BlockSpec((tk, tn), lambda i, j, kk: (kk, j))]
    operands = [a, w]
    for arr, off in extras:
        assert off % tn == 0, (off, tn)
        in_specs.append(pl.BlockSpec((tm, tn), lambda i, j, kk, o=off // tn: (i, j + o)))
        operands.append(arr)
    return pl.pallas_call(
        functools.partial(_mm_kernel, n_extra=len(extras), epilogue=epilogue, nk=nk),
        grid=(m // tm, n // tn, nk),
        in_specs=in_specs,
        out_specs=pl.BlockSpec((tm, tn), lambda i, j, kk: (i, j)),
        out_shape=jax.ShapeDtypeStruct((m, n), out_dtype),
        scratch_shapes=[pltpu.VMEM((tm, tn), F32)] if nk > 1 else [],
        compiler_params=_params(("parallel", "parallel", "arbitrary")),
        name=name,
    )(*operands)


def _token_shift_mix(x, carry_ref, mu):
    rows = lax.broadcasted_iota(jnp.int32, x.shape, 0)
    prev = jnp.where(rows == 0, carry_ref[SUBLANES - 1:SUBLANES, :], pltpu.roll(x, 1, 0))
    carry_ref[...] = x[x.shape[0] - SUBLANES:, :]
    return x + (prev - x) * mu


def _rwkv_kernel(r_ref, k_ref, v_ref, wd_ref, ad_ref, gd_ref,
                 mur_ref, muk_ref, muv_ref, muw_ref, mua_ref, mug_ref,
                 w0_ref, wup_ref, a0_ref, aup_ref, gup_ref,
                 kk_ref, ka_ref, rk_ref, lnw_ref, lnb_ref,
                 o_ref,
                 cr_ref, ck_ref, cv_ref, cw_ref, ca_ref, cg_ref,
                 r_s, km_s, v_s, lw_s, kn_s, a_s, y_s, state_ref,
                 *, head, chunk):
    tt = r_ref.shape[0]
    heads_per_block = r_ref.shape[1] // head
    c = chunk

    @pl.when(pl.program_id(2) == 0)
    def _():
        for ref in (cr_ref, ck_ref, cv_ref, cw_ref, ca_ref, cg_ref, state_ref):
            ref[...] = jnp.zeros(ref.shape, ref.dtype)

    r = _token_shift_mix(r_ref[...], cr_ref, mur_ref[...])
    k = _token_shift_mix(k_ref[...], ck_ref, muk_ref[...])
    v = _token_shift_mix(v_ref[...], cv_ref, muv_ref[...])
    wd = _token_shift_mix(wd_ref[...], cw_ref, muw_ref[...])
    ad = _token_shift_mix(ad_ref[...], ca_ref, mua_ref[...])
    gd = _token_shift_mix(gd_ref[...], cg_ref, mug_ref[...])

    w_lin = w0_ref[...] + _dot(jnp.tanh(wd).astype(BF16), wup_ref[...])
    w = -_softplus(-w_lin) - 0.5
    log_decay = -jnp.exp(w)
    a = _sigmoid(a0_ref[...] + _dot(ad.astype(BF16), aup_ref[...]))
    g = _dot(_sigmoid(gd).astype(BF16), gup_ref[...])
    k_mod = k * (1.0 + (a - 1.0) * ka_ref[...])
    kk_raw = k * kk_ref[...]
    kn_parts = []
    for h in range(heads_per_block):
        part = kk_raw[:, h * head:(h + 1) * head]
        norm = jnp.sqrt(jnp.sum(part * part, axis=-1, keepdims=True))
        kn_parts.append(part / jnp.maximum(norm, 1e-12))
    kn = jnp.concatenate(kn_parts, axis=1)

    r_s[...] = r
    km_s[...] = k_mod
    v_s[...] = v
    lw_s[...] = log_decay
    kn_s[...] = kn
    a_s[...] = a

    row = lax.broadcasted_iota(jnp.int32, (c, c), 0)
    col = lax.broadcasted_iota(jnp.int32, (c, c), 1)
    incl = row >= col
    strict = row > col
    tri_incl = incl.astype(F32)
    same_block = (row // RWKV_SOLVE_BLOCK) == (col // RWKV_SOLVE_BLOCK)
    eye_c = (row == col).astype(F32)
    hrow = lax.broadcasted_iota(jnp.int32, (head, head), 0)
    hcol = lax.broadcasted_iota(jnp.int32, (head, head), 1)
    eye_h = hrow == hcol
    mm = functools.partial(_dot, precision=HIGHEST)

    def chunk_step(ci, carry):
        c0 = pl.multiple_of(ci * c, c)
        rows = pl.ds(c0, c)
        rr, km, vv, lw, kn_c, aa = (s[rows, :] for s in (r_s, km_s, v_s, lw_s, kn_s, a_s))
        cs = mm(tri_incl, lw)
        cs_last = cs[c - 1:c, :]
        e_neg = jnp.exp(-cs)
        e_end = jnp.exp(cs_last - cs)
        kb = kn_c * aa
        rt = rr * jnp.exp(cs)
        at = -kn_c * jnp.exp(cs - lw)
        bt = kb * e_neg
        kt = km * e_neg
        bg = kb * e_end
        kg = km * e_end
        gam = jnp.exp(cs_last)
        y_parts = []
        for h in range(heads_per_block):
            sl = slice(h * head, (h + 1) * head)
            v_h = vv[:, sl]
            left = jnp.concatenate([at[:, sl], rt[:, sl]], axis=0)
            right = jnp.concatenate([bt[:, sl], kt[:, sl]], axis=0)
            gram = _dot_nt(left, right, precision=HIGHEST)
            a_ab = jnp.where(strict, gram[:c, :c], 0.0)
            a_ak = jnp.where(strict, gram[:c, c:], 0.0)
            a_rb = jnp.where(incl, gram[c:, :c], 0.0)
            a_rk = jnp.where(incl, gram[c:, c:], 0.0)
            n_d = jnp.where(same_block, a_ab, 0.0)
            n_o = a_ab - n_d
            n_2 = mm(n_d, n_d)
            n_4 = mm(n_2, n_2)
            n_8 = mm(n_4, n_4)
            t_d = mm(mm(eye_c + n_d, eye_c + n_2), mm(eye_c + n_4, eye_c + n_8))
            x_1 = mm(t_d, n_o)
            x_2 = mm(x_1, x_1)
            t_inv = mm(eye_c + x_1, mm(eye_c + x_2, t_d))
            pq = mm(t_inv, jnp.concatenate([at[:, sl], mm(a_ak, v_h)], axis=1))
            rb_pq = mm(a_rb, pq)
            r_eff = rt[:, sl] + rb_pq[:, :head]
            y_in = rb_pq[:, head:] + mm(a_rk, v_h)
            bg_pq = _dot_tn(bg[:, sl], pq, precision=HIGHEST)
            m_state = jnp.where(eye_h, jnp.broadcast_to(gam[:, sl], (head, head)), 0.0) + bg_pq[:, :head]
            z_state = bg_pq[:, head:] + _dot_tn(kg[:, sl], v_h, precision=HIGHEST)
            st = state_ref[h]
            y_parts.append(mm(r_eff, st) + y_in)
            state_ref[h] = mm(m_state, st) + z_state
        y_s[rows, :] = jnp.concatenate(y_parts, axis=1)
        return carry

    lax.fori_loop(0, tt // c, chunk_step, 0)

    y = y_s[...]
    rkk = r * k_mod * rk_ref[...]
    out_parts = []
    for h in range(heads_per_block):
        sl = slice(h * head, (h + 1) * head)
        y_h = y[:, sl]
        mean = jnp.mean(y_h, axis=-1, keepdims=True)
        var = jnp.mean(jnp.square(y_h - mean), axis=-1, keepdims=True)
        yn = (y_h - mean) * lax.rsqrt(var + RWKV_LN_EPS)
        bonus = jnp.sum(rkk[:, sl], axis=-1, keepdims=True) * v[:, sl]
        out_parts.append((yn, bonus))
    yn = jnp.concatenate([p[0] for p in out_parts], axis=1)
    bonus = jnp.concatenate([p[1] for p in out_parts], axis=1)
    o_ref[...] = ((yn * lnw_ref[...] + lnb_ref[...] + bonus) * g).astype(o_ref.dtype)


def _rwkv_branch(p_rkv, p_small, lora_off, bsz, t_len, mu, w0, w_up, a0, a_up, g_up,
                 k_k, k_a, r_k, ln_w, ln_b):
    width = w0.shape[-1]
    head = r_k.shape[-1]
    wl, al, gl = w_up.shape[0], a_up.shape[0], g_up.shape[0]
    wp, ap, gp = (_round_up(n, LANES) for n in (wl, al, gl))
    off_g, off_w, off_a = lora_off
    blk = LANES
    n_blk = width // blk
    tt = _pick(t_len, (512, 256, 128, 64))
    chunk = min(RWKV_CHUNK, tt)
    nt = t_len // tt
    mu_r, mu_k, mu_v = (mu[i * width:(i + 1) * width].reshape(1, width) for i in range(3))
    o = 3 * width
    mu_w = _pad_to(mu[o:o + wl], 0, wp).reshape(1, wp)
    mu_a = _pad_to(mu[o + wl:o + wl + al], 0, ap).reshape(1, ap)
    mu_g = _pad_to(mu[o + wl + al:o + wl + al + gl], 0, gp).reshape(1, gp)
    w_up_p = _pad_to(w_up, 0, wp).astype(BF16)
    a_up_p = _pad_to(a_up, 0, ap).astype(BF16)
    g_up_p = _pad_to(g_up, 0, gp).astype(BF16)
    vec = lambda a: a.reshape(1, width).astype(F32)

    row_blk = lambda b, p, t: b * nt + t
    col_spec = lambda j: pl.BlockSpec((tt, blk), lambda b, p, t, j=j: (row_blk(b, p, t), j * n_blk + p))
    small_spec = lambda w_, off: pl.BlockSpec((tt, w_), lambda b, p, t, o_=off // w_: (row_blk(b, p, t), o_))
    const_spec = lambda w_: pl.BlockSpec((1, w_), lambda b, p, t: (0, 0))
    chan_spec = pl.BlockSpec((1, blk), lambda b, p, t: (0, p))
    up_spec = lambda rows: pl.BlockSpec((rows, blk), lambda b, p, t: (0, p))
    assert off_w % wp == 0 and off_a % ap == 0 and off_g % gp == 0

    heads_per_block = blk // head
    return pl.pallas_call(
        functools.partial(_rwkv_kernel, head=head, chunk=chunk),
        grid=(bsz, n_blk, nt),
        in_specs=[col_spec(0), col_spec(1), col_spec(2),
                  small_spec(wp, off_w), small_spec(ap, off_a), small_spec(gp, off_g),
                  chan_spec, chan_spec, chan_spec, const_spec(wp), const_spec(ap), const_spec(gp),
                  chan_spec, up_spec(wp), chan_spec, up_spec(ap), up_spec(gp),
                  chan_spec, chan_spec, chan_spec, chan_spec, chan_spec],
        out_specs=pl.BlockSpec((tt, blk), lambda b, p, t: (row_blk(b, p, t), p)),
        out_shape=jax.ShapeDtypeStruct((bsz * t_len, width), BF16),
        scratch_shapes=[pltpu.VMEM((SUBLANES, blk), F32)] * 3
                       + [pltpu.VMEM((SUBLANES, wp), F32), pltpu.VMEM((SUBLANES, ap), F32),
                          pltpu.VMEM((SUBLANES, gp), F32)]
                       + [pltpu.VMEM((tt, blk), F32)] * 7
                       + [pltpu.VMEM((heads_per_block, head, head), F32)],
        compiler_params=_params(("parallel", "parallel", "arbitrary")),
        name="rwkv7_mixer",
    )(p_rkv, p_rkv, p_rkv, p_small, p_small, p_small,
      mu_r, mu_k, mu_v, mu_w, mu_a, mu_g,
      vec(w0), w_up_p, vec(a0), a_up_p, g_up_p,
      vec(k_k), vec(k_a), vec(r_k), vec(ln_w), vec(ln_b))


def _dt_kernel(dt_ref, bias_ref, alog_ref, dt_out_ref, acs_out_ref):
    n = dt_ref.shape[0]
    dt = _softplus(dt_ref[...] + bias_ref[...])
    a = dt * (-jnp.exp(alog_ref[...]))
    row = lax.broadcasted_iota(jnp.int32, (n, n), 0)
    col = lax.broadcasted_iota(jnp.int32, (n, n), 1)
    dt_out_ref[...] = dt
    acs_out_ref[...] = _dot((row >= col).astype(F32), a, precision=HIGHEST)


def _dt_tables(p_small, dt_off, hp, dt_bias, a_log):
    m = p_small.shape[0]
    spec = pl.BlockSpec((SSM_CHUNK, hp), lambda i: (i, 0))
    const = pl.BlockSpec((1, hp), lambda i: (0, 0))
    assert dt_off % hp == 0
    return pl.pallas_call(
        _dt_kernel,
        grid=(m // SSM_CHUNK,),
        in_specs=[pl.BlockSpec((SSM_CHUNK, hp), lambda i, o=dt_off // hp: (i, o)), const, const],
        out_specs=[spec, spec],
        out_shape=[jax.ShapeDtypeStruct((m, hp), F32)] * 2,
        compiler_params=_params(("parallel",)),
        name="ssd_dt_tables",
    )(p_small, _pad_to(dt_bias, 0, hp).reshape(1, hp).astype(F32),
      _pad_to(a_log, 0, hp).reshape(1, hp).astype(F32))


def _causal_conv(x, tail_ref, w_ref, b_ref):
    n, width = x.shape
    k_len = w_ref.shape[0]
    tail = tail_ref[...]
    rows8 = lax.broadcasted_iota(jnp.int32, (SUBLANES, width), 0)
    acc = x * w_ref[k_len - 1:k_len, :] + b_ref[...]
    for shift in range(1, k_len):
        rolled = pltpu.roll(x, shift, 0)
        top = jnp.where(rows8 < shift, pltpu.roll(tail, shift, 0), rolled[:SUBLANES, :])
        shifted = jnp.concatenate([top, rolled[SUBLANES:, :]], axis=0)
        acc = acc + shifted * w_ref[k_len - 1 - shift:k_len - shift, :]
    tail_ref[...] = x[n - SUBLANES:, :]
    return acc


def _ssd_kernel(z_ref, x_ref, b_ref, c_ref, dt_ref, acs_ref, acst_ref,
                cwx_ref, cwb_ref, cwc_ref, cbx_ref, cbb_ref, cbc_ref, dskip_ref, nw_ref,
                o_ref,
                tailx_ref, tailb_ref, tailc_ref, state_ref, *, headdim):
    n = x_ref.shape[0]
    gw = x_ref.shape[1]
    hpg = gw // headdim

    @pl.when(pl.program_id(2) == 0)
    def _():
        for ref in (tailx_ref, tailb_ref, tailc_ref, state_ref):
            ref[...] = jnp.zeros(ref.shape, ref.dtype)

    xg = _silu(_causal_conv(x_ref[...], tailx_ref, cwx_ref, cbx_ref))
    bm = _silu(_causal_conv(b_ref[...], tailb_ref, cwb_ref, cbb_ref))
    cm = _silu(_causal_conv(c_ref[...], tailc_ref, cwc_ref, cbc_ref))
    dt = dt_ref[0, 0]
    acs = acs_ref[0, 0]
    acs_t = acst_ref[0, 0]

    expand = (lax.broadcasted_iota(jnp.int32, (hpg, gw), 0)
              == lax.broadcasted_iota(jnp.int32, (hpg, gw), 1) // headdim).astype(F32)
    dt_x = _dot(dt, expand, precision=HIGHEST)
    acs_x = _dot(acs, expand, precision=HIGHEST)
    last_x = acs_x[n - 1:n, :]
    xd = xg * dt_x
    bm16 = bm.astype(BF16)
    cm16 = cm.astype(BF16)
    cb = _dot_nt(cm16, bm16)
    state = state_ref[...]
    y = _dot(cm16, state.astype(BF16)) * jnp.exp(acs_x)

    row = lax.broadcasted_iota(jnp.int32, (n, n), 0)
    col = lax.broadcasted_iota(jnp.int32, (n, n), 1)
    causal = row >= col
    parts = []
    for e in range(hpg):
        seg = jnp.where(causal, jnp.exp(acs[:, e:e + 1] - acs_t[e:e + 1, :]), 0.0)
        parts.append(_dot((cb * seg).astype(BF16), xd[:, e * headdim:(e + 1) * headdim].astype(BF16)))
    y = y + jnp.concatenate(parts, axis=1) + xg * dskip_ref[...]

    state_ref[...] = state * jnp.exp(last_x) + _dot_tn(bm16, (xd * jnp.exp(last_x - acs_x)).astype(BF16))

    y = y * _silu(z_ref[...])
    y = y * lax.rsqrt(jnp.mean(y * y, axis=-1, keepdims=True) + RMS_EPS)
    o_ref[...] = (y * nw_ref[...]).astype(o_ref.dtype)


def _ssd_branch(p_zx, p_small, bc_off, dt_sp, acs, bsz, t_len, conv_w, conv_b, d_skip, norm_w):
    inner = norm_w.shape[-1]
    heads = d_skip.shape[-1]
    headdim = inner // heads
    groups, n_state = SSM_GROUPS, SSM_STATE
    gw = inner // groups
    hpg = heads // groups
    n = SSM_CHUNK
    nt = t_len // n
    off_b, off_c = bc_off
    assert off_b % n_state == 0 and off_c % n_state == 0
    k_len = conv_w.shape[0]
    gn = groups * n_state
    cw_x, cw_b, cw_c = conv_w[:, :inner], conv_w[:, inner:inner + gn], conv_w[:, inner + gn:]
    cb_x, cb_b, cb_c = (conv_b[s].reshape(1, -1) for s in
                        (slice(0, inner), slice(inner, inner + gn), slice(inner + gn, inner + 2 * gn)))
    per_group = lambda t: t[:, :heads].reshape(bsz, t_len, groups, hpg).transpose(0, 2, 1, 3)
    dt_g = per_group(dt_sp)
    acs_g = per_group(acs)
    acs_gt = acs_g.transpose(0, 1, 3, 2)
    d_exp = jnp.repeat(d_skip.astype(F32), headdim).reshape(1, inner)

    rows = lambda b, g, t: b * nt + t
    tab_spec = pl.BlockSpec((1, 1, n, hpg), lambda b, g, t: (b, g, t, 0))
    return pl.pallas_call(
        functools.partial(_ssd_kernel, headdim=headdim),
        grid=(bsz, groups, nt),
        in_specs=[pl.BlockSpec((n, gw), lambda b, g, t: (rows(b, g, t), g)),
                  pl.BlockSpec((n, gw), lambda b, g, t: (rows(b, g, t), groups + g)),
                  pl.BlockSpec((n, n_state), lambda b, g, t, o=off_b // n_state: (rows(b, g, t), o + g)),
                  pl.BlockSpec((n, n_state), lambda b, g, t, o=off_c // n_state: (rows(b, g, t), o + g)),
                  tab_spec, tab_spec,
                  pl.BlockSpec((1, 1, hpg, n), lambda b, g, t: (b, g, 0, t)),
                  pl.BlockSpec((k_len, gw), lambda b, g, t: (0, g)),
                  pl.BlockSpec((k_len, n_state), lambda b, g, t: (0, g)),
                  pl.BlockSpec((k_len, n_state), lambda b, g, t: (0, g)),
                  pl.BlockSpec((1, gw), lambda b, g, t: (0, g)),
                  pl.BlockSpec((1, n_state), lambda b, g, t: (0, g)),
                  pl.BlockSpec((1, n_state), lambda b, g, t: (0, g)),
                  pl.BlockSpec((1, gw), lambda b, g, t: (0, g)),
                  pl.BlockSpec((1, gw), lambda b, g, t: (0, g))],
        out_specs=pl.BlockSpec((n, gw), lambda b, g, t: (rows(b, g, t), g)),
        out_shape=jax.ShapeDtypeStruct((bsz * t_len, inner), BF16),
        scratch_shapes=[pltpu.VMEM((SUBLANES, gw), F32), pltpu.VMEM((SUBLANES, n_state), F32),
                        pltpu.VMEM((SUBLANES, n_state), F32), pltpu.VMEM((n_state, gw), F32)],
        compiler_params=_params(("parallel", "parallel", "arbitrary")),
        name="mamba2_ssd",
    )(p_zx, p_zx, p_small, p_small, dt_g, acs_g, acs_gt,
      cw_x, cw_b, cw_c, cb_x, cb_b, cb_c, d_exp, norm_w.reshape(1, inner).astype(F32))


def _gated(acc, gate):
    return _sigmoid(gate) * acc


def _gated_add(acc, gate, other):
    return other + _sigmoid(gate) * acc


def _residual(acc, res):
    return res + acc


def _relu_sq(acc):
    return jnp.square(jnp.maximum(acc, 0.0))


def kernel(x, norm_mix_w, w_in, rwkv_mu, rwkv_w0, rwkv_w_up, rwkv_a0, rwkv_a_up, rwkv_g_up, rwkv_k_k, rwkv_k_a, rwkv_r_k, rwkv_ln_w, rwkv_ln_b, ssm_conv_w, ssm_conv_b, ssm_dt_bias, ssm_a_log, ssm_d, ssm_norm_w, w_br_rwkv, w_br_ssm, w_out, norm_mlp_w, w_mlp_up, w_mlp_down, norm_final_w):
    bsz, t_len, d = x.shape
    m = bsz * t_len
    depth = w_in.shape[0]
    width = rwkv_w0.shape[-1]
    wl, al, gl = rwkv_w_up.shape[1], rwkv_a_up.shape[1], rwkv_g_up.shape[1]
    wp, ap, gp = (_round_up(n, LANES) for n in (wl, al, gl))
    rwkv_cols = 3 * width + wl + al + gl
    inner = ssm_norm_w.shape[-1]
    heads = ssm_a_log.shape[-1]
    hp = _round_up(heads, LANES)
    gn = SSM_GROUPS * SSM_STATE

    h = x.reshape(m, d).astype(F32)
    for layer in range(depth):
        w = w_in[layer]
        o_z = rwkv_cols
        o_x = o_z + inner
        o_b = o_x + inner
        o_c = o_b + gn
        o_dt = o_c + gn
        o_ga = o_dt + heads
        o_wd = 3 * width
        o_ad = o_wd + wl
        o_gd = o_ad + al
        w_rkv = w[:, :3 * width].astype(BF16)
        w_zx = w[:, o_z:o_b].astype(BF16)
        w_gate = w[:, o_ga:o_ga + 2 * d].astype(BF16)
        small = [w[:, o_b:o_dt],
                 _pad_to(w[:, o_gd:o_gd + gl], 1, gp),
                 _pad_to(w[:, o_dt:o_dt + heads], 1, hp),
                 _pad_to(w[:, o_wd:o_wd + wl], 1, wp),
                 _pad_to(w[:, o_ad:o_ad + al], 1, ap)]
        off_g = 2 * gn
        off_dt = off_g + gp
        off_w = off_dt + hp
        off_a = off_w + wp
        n_small = _round_up(off_a + ap, 1024)
        w_small = _pad_to(jnp.concatenate(small, axis=1), 1, n_small).astype(BF16)

        u = _rmsnorm(h, norm_mix_w[layer], BF16)
        p_rkv = _matmul(u, w_rkv, F32, tk_max=4096, name="proj_rkv")
        p_zx = _matmul(u, w_zx, F32, tk_max=4096, name="proj_zx")
        p_gate = _matmul(u, w_gate, F32, tk_max=4096, name="proj_gate")
        p_small = _matmul(u, w_small, F32, tk_max=4096, name="proj_small")

        y_a = _rwkv_branch(p_rkv, p_small, (off_g, off_w, off_a), bsz, t_len,
                           rwkv_mu[layer], rwkv_w0[layer], rwkv_w_up[layer], rwkv_a0[layer],
                           rwkv_a_up[layer], rwkv_g_up[layer], rwkv_k_k[layer], rwkv_k_a[layer],
                           rwkv_r_k[layer], rwkv_ln_w[layer], rwkv_ln_b[layer])

        dt_sp, acs = _dt_tables(p_small, off_dt, hp, ssm_dt_bias[layer], ssm_a_log[layer])
        y_b = _ssd_branch(p_zx, p_small, (0, gn), dt_sp, acs, bsz, t_len,
                          ssm_conv_w[layer], ssm_conv_b[layer], ssm_d[layer], ssm_norm_w[layer])

        part = _matmul(y_a, w_br_rwkv[layer].astype(BF16), F32, epilogue=_gated,
                       extras=((p_gate, 0),), name="branch_rwkv")
        merged = _matmul(y_b, w_br_ssm[layer].astype(BF16), BF16, epilogue=_gated_add,
                         extras=((p_gate, d), (part, 0)), name="branch_ssm")
        h = _matmul(merged, w_out[layer].astype(BF16), F32, epilogue=_residual,
                    extras=((h, 0),), name="out_proj")

        u = _rmsnorm(h, norm_mlp_w[layer], BF16)
        hidden = _matmul(u, w_mlp_up[layer].astype(BF16), BF16, epilogue=_relu_sq,
                         tk_max=4096, name="mlp_up")
        h = _matmul(hidden, w_mlp_down[layer].astype(BF16), F32, epilogue=_residual,
                    extras=((h, 0),), name="mlp_down")
    return _rmsnorm(h, norm_final_w, x.dtype).reshape(bsz, t_len, d)
```

```python
import functools

import jax
import jax.numpy as jnp
from jax import lax
from jax.experimental import pallas as pl
from jax.experimental.pallas import tpu as pltpu

F32 = jnp.float32
BF16 = jnp.bfloat16
HIGHEST = lax.Precision.HIGHEST

LANES = 128
SUBLANES = 8
VMEM_LIMIT_BYTES = 56 * 1024 * 1024
MATMUL_VMEM_BUDGET = 44 * 1024 * 1024
MATMUL_MIN_REUSE = 256

RMS_EPS = 1e-5
RWKV_LN_EPS = 64e-5
RWKV_CHUNK = 64
RWKV_SOLVE_BLOCK = 16
RWKV_BLOCK_LANES = 512
RWKV_GROUP_LANES = 256
RWKV_UNROLL = 8
SSM_GROUPS = 8
SSM_STATE = 128
SSM_CHUNK = 128


def _round_up(n, m):
    return (n + m - 1) // m * m


def _pick(dim, prefs):
    for p in prefs:
        if dim % p == 0:
            return p
    return dim


def _pad_to(a, axis, size):
    if a.shape[axis] == size:
        return a
    pad = [(0, 0)] * a.ndim
    pad[axis] = (0, size - a.shape[axis])
    return jnp.pad(a, pad)


def _params(sem):
    return pltpu.CompilerParams(dimension_semantics=sem, vmem_limit_bytes=VMEM_LIMIT_BYTES)


def _dot(a, b, precision=None):
    return jnp.dot(a, b, preferred_element_type=F32, precision=precision)


def _dot_nt(a, b, precision=None):
    return lax.dot_general(a, b, (((1,), (1,)), ((), ())), preferred_element_type=F32,
                           precision=precision)


def _dot_tn(a, b, precision=None):
    return lax.dot_general(a, b, (((0,), (0,)), ((), ())), preferred_element_type=F32,
                           precision=precision)


def _sigmoid(x):
    return 1.0 / (1.0 + jnp.exp(-x))


def _silu(x):
    return x * _sigmoid(x)


def _softplus(x):
    return jnp.maximum(x, 0.0) + jnp.log(1.0 + jnp.exp(-jnp.abs(x)))


def _rmsnorm_kernel(x_ref, w_ref, o_ref):
    x = x_ref[...]
    y = x * lax.rsqrt(jnp.mean(x * x, axis=-1, keepdims=True) + RMS_EPS)
    o_ref[...] = (y * w_ref[...]).astype(o_ref.dtype)


def _rmsnorm(x2d, w, out_dtype):
    m, d = x2d.shape
    tr = _pick(m, (256, 128, 64, 32, 16, 8))
    return pl.pallas_call(
        _rmsnorm_kernel,
        grid=(m // tr,),
        in_specs=[pl.BlockSpec((tr, d), lambda i: (i, 0)),
                  pl.BlockSpec((1, d), lambda i: (0, 0))],
        out_specs=pl.BlockSpec((tr, d), lambda i: (i, 0)),
        out_shape=jax.ShapeDtypeStruct((m, d), out_dtype),
        compiler_params=_params(("parallel",)),
        name="rmsnorm",
    )(x2d, w.reshape(1, d).astype(F32))


def _mm_kernel(*refs, n_extra, epilogue, nk):
    a_ref, w_ref = refs[0], refs[1]
    extras = refs[2:2 + n_extra]
    o_ref = refs[2 + n_extra]
    part = _dot(a_ref[...], w_ref[...].astype(BF16))
    if nk == 1:
        o_ref[...] = epilogue(part, *[e[...] for e in extras]).astype(o_ref.dtype)
        return
    acc_ref = refs[3 + n_extra]
    k = pl.program_id(2)

    @pl.when(k == 0)
    def _():
        acc_ref[...] = part

    @pl.when(k > 0)
    def _():
        acc_ref[...] += part

    @pl.when(k == nk - 1)
    def _():
        o_ref[...] = epilogue(acc_ref[...], *[e[...] for e in extras]).astype(o_ref.dtype)


def _matmul_tiles(m, k, n, w_bytes, out_bytes, extras):
    best = None
    for tk in (k, k // 2, k // 4, k // 8):
        if tk < 1 or k % tk or (tk != k and tk % LANES):
            continue
        for tm in (1024, 512, 256, 128, 64, 32, 16, 8):
            for tn in (1024, 512, 256, 128):
                if m % tm or n % tn or any(off % tn for _, off in extras):
                    continue
                vmem = 2 * tk * (2 * tm + w_bytes * tn) + 2 * tm * tn * (out_bytes + 4 * len(extras))
                vmem += 4 * tm * tn if tk != k else 0
                vmem += 2 * tk * tn if w_bytes != 2 else 0
                if vmem > MATMUL_VMEM_BUDGET:
                    continue
                reuse = tm * tn / (tm + tn)
                score = (reuse >= MATMUL_MIN_REUSE, tk == k, reuse, tk, tm)
                if best is None or score > best[0]:
                    best = (score, (tm, tn, tk))
    assert best is not None, (m, k, n)
    return best[1]


def _matmul(a, w, out_dtype, epilogue=None, extras=(), n_cols=None, name="matmul"):
    m, k = a.shape
    n = w.shape[1] if n_cols is None else n_cols
    if epilogue is None:
        epilogue = lambda acc: acc
    tm, tn, tk = _matmul_tiles(m, k, n, w.dtype.itemsize, jnp.dtype(out_dtype).itemsize, extras)
    nk = k // tk
    in_specs = [pl.BlockSpec((tm, tk), lambda i, j, kk: (i, kk)),
                pl.BlockSpec((tk, tn), lambda i, j, kk: (kk, j))]
    operands = [a, w]
    for arr, off in extras:
        assert off % tn == 0, (off, tn)
        in_specs.append(pl.BlockSpec((tm, tn), lambda i, j, kk, o=off // tn: (i, j + o)))
        operands.append(arr)
    return pl.pallas_call(
        functools.partial(_mm_kernel, n_extra=len(extras), epilogue=epilogue, nk=nk),
        grid=(m // tm, n // tn, nk),
        in_specs=in_specs,
        out_specs=pl.BlockSpec((tm, tn), lambda i, j, kk: (i, j)),
        out_shape=jax.ShapeDtypeStruct((m, n), out_dtype),
        scratch_shapes=[pltpu.VMEM((tm, tn), F32)] if nk > 1 else [],
        compiler_params=_params(("parallel", "parallel", "arbitrary")),
        name=name,
    )(*operands)


def _transpose_kernel(wt_hbm, o_ref, buf_ref, sem_ref, *, row0, n_col_blocks, n_steps):
    rows, wb = o_ref.shape
    step = pl.program_id(0) * n_col_blocks + pl.program_id(1)

    def window(s, slot):
        r0 = pl.multiple_of(row0 + (s % n_col_blocks) * wb, SUBLANES)
        c0 = pl.multiple_of((s // n_col_blocks) * rows, LANES)
        return pltpu.make_async_copy(wt_hbm.at[pl.ds(r0, wb), pl.ds(c0, rows)],
                                     buf_ref.at[slot], sem_ref.at[slot])

    @pl.when(step == 0)
    def _():
        window(step, 0).start()

    @pl.when(step + 1 < n_steps)
    def _():
        window(step + 1, (step + 1) % 2).start()

    window(step, step % 2).wait()
    o_ref[...] = buf_ref[step % 2].T.astype(o_ref.dtype)


def _transposed_cast(wt, row_start, n_rows):
    k = wt.shape[1]
    assert row_start % SUBLANES == 0 and n_rows % LANES == 0 and k % LANES == 0
    wb = next(c for c in range(2048, 0, -LANES) if n_rows % c == 0)
    rows = _pick(k, (512, 256, 128))
    n_r, n_c = k // rows, n_rows // wb
    return pl.pallas_call(
        functools.partial(_transpose_kernel, row0=row_start, n_col_blocks=n_c, n_steps=n_r * n_c),
        grid=(n_r, n_c),
        in_specs=[pl.BlockSpec(memory_space=pl.ANY)],
        out_specs=pl.BlockSpec((rows, wb), lambda i, j: (i, j)),
        out_shape=jax.ShapeDtypeStruct((k, n_rows), BF16),
        scratch_shapes=[pltpu.VMEM((2, wb, rows), wt.dtype), pltpu.SemaphoreType.DMA((2,))],
        compiler_params=_params(("arbitrary", "arbitrary")),
        name="transposed_cast",
    )(wt)


def _token_shift_mix(x, carry_ref, mu):
    rolled = pltpu.roll(x, 1, 0)
    first = lax.broadcasted_iota(jnp.int32, (SUBLANES, x.shape[1]), 0) == 0
    top = jnp.where(first, carry_ref[SUBLANES - 1:SUBLANES, :], rolled[:SUBLANES, :])
    prev = jnp.concatenate([top, rolled[SUBLANES:, :]], axis=0)
    carry_ref[...] = x[x.shape[0] - SUBLANES:, :]
    return x + (prev - x) * mu


def _chunk_cumsum(x, chunk):
    rows = lax.broadcasted_iota(jnp.int32, x.shape, 0) & (chunk - 1)
    shift = 1
    while shift < chunk:
        x = x + jnp.where(rows >= shift, pltpu.roll(x, shift, 0), 0.0)
        shift *= 2
    return x


def _mm(a, b):
    return _dot(a.astype(BF16), b.astype(BF16))


def _mm_nt(a, b):
    return _dot_nt(a.astype(BF16), b.astype(BF16))


def _mm_tn(a, b):
    return _dot_tn(a.astype(BF16), b.astype(BF16))


def _lora_kernel(wd_ref, ad_ref, gd_ref, muw_ref, mua_ref, mug_ref, wt_ref, at_ref, gs_ref,
                 cw_ref, ca_ref, cg_ref):
    @pl.when(pl.program_id(1) == 0)
    def _():
        for ref in (cw_ref, ca_ref, cg_ref):
            ref[...] = jnp.zeros(ref.shape, ref.dtype)

    wt_ref[...] = jnp.tanh(_token_shift_mix(wd_ref[...], cw_ref, muw_ref[...])).astype(wt_ref.dtype)
    at_ref[...] = _token_shift_mix(ad_ref[...], ca_ref, mua_ref[...]).astype(at_ref.dtype)
    gs_ref[...] = _sigmoid(_token_shift_mix(gd_ref[...], cg_ref, mug_ref[...])).astype(gs_ref.dtype)


def _lora_inputs(p_small, offs, widths, mus, bsz, t_len):
    m = p_small.shape[0]
    tt = _pick(t_len, (512, 256, 128, 64))
    nt = t_len // tt
    for off, w_ in zip(offs, widths):
        assert off % w_ == 0, (off, w_)
    in_spec = lambda w_, off: pl.BlockSpec((tt, w_), lambda b, t, o_=off // w_: (b * nt + t, o_))
    out_spec = lambda w_: pl.BlockSpec((tt, w_), lambda b, t: (b * nt + t, 0))
    const = lambda w_: pl.BlockSpec((1, w_), lambda b, t: (0, 0))
    return pl.pallas_call(
        _lora_kernel,
        grid=(bsz, nt),
        in_specs=[in_spec(w_, off) for w_, off in zip(widths, offs)] + [const(w_) for w_ in widths],
        out_specs=[out_spec(w_) for w_ in widths],
        out_shape=[jax.ShapeDtypeStruct((m, w_), BF16) for w_ in widths],
        scratch_shapes=[pltpu.VMEM((SUBLANES, w_), F32) for w_ in widths],
        compiler_params=_params(("parallel", "arbitrary")),
        name="rwkv7_lora_inputs",
    )(p_small, p_small, p_small, *mus)


def _rwkv_kernel(r_ref, k_ref, v_ref, wt_ref, at_ref, gs_ref,
                 mur_ref, muk_ref, muv_ref,
                 w0_ref, wup_ref, a0_ref, aup_ref, gup_ref,
                 kk_ref, ka_ref, rk_ref, lnw_ref, lnb_ref,
                 o_ref,
                 cr_ref, ck_ref, cv_ref,
                 r_s, km_s, v_s, lw_s, kn_s, a_s, cs_s, rp_s, yi_s, m_s, z_s, state_ref,
                 *, head, chunk, unroll, group):
    tt, blk = r_ref.shape
    gl = group
    n_groups = blk // gl
    hpb = gl // head
    c = chunk
    n_chunks = tt // c

    @pl.when(pl.program_id(2) == 0)
    def _():
        for ref in (cr_ref, ck_ref, cv_ref, state_ref):
            ref[...] = jnp.zeros(ref.shape, ref.dtype)

    lane = lax.broadcasted_iota(jnp.int32, (1, gl), 1)
    brow = lax.broadcasted_iota(jnp.int32, (gl, gl), 0)
    bcol = lax.broadcasted_iota(jnp.int32, (gl, gl), 1)
    same_head = (brow // head) == (bcol // head)
    eye_blk = brow == bcol
    ones_heads = same_head.astype(BF16)

    def head_sum(x):
        x16 = x.astype(BF16)
        return jnp.concatenate([_dot(x16[:, q * gl:(q + 1) * gl], ones_heads) for q in range(n_groups)], axis=1)

    r = _token_shift_mix(r_ref[...], cr_ref, mur_ref[...])
    k = _token_shift_mix(k_ref[...], ck_ref, muk_ref[...])
    v = _token_shift_mix(v_ref[...], cv_ref, muv_ref[...])

    w_lin = w0_ref[...] + _dot(wt_ref[...], wup_ref[...])
    w = -_softplus(-w_lin) - 0.5
    log_decay = -jnp.exp(w)
    a = _sigmoid(a0_ref[...] + _dot(at_ref[...], aup_ref[...]))
    gate = _dot(gs_ref[...], gup_ref[...])
    k_mod = k * (1.0 + (a - 1.0) * ka_ref[...])
    kk_raw = k * kk_ref[...]
    kn = kk_raw / jnp.maximum(jnp.sqrt(head_sum(kk_raw * kk_raw)), 1e-12)

    r_s[...] = r
    km_s[...] = k_mod
    v_s[...] = v
    lw_s[...] = log_decay
    kn_s[...] = kn
    a_s[...] = a
    cs_s[...] = _chunk_cumsum(log_decay, c)

    wide = hpb * c
    row = lax.broadcasted_iota(jnp.int32, (c, wide), 0)
    col = lax.broadcasted_iota(jnp.int32, (c, wide), 1) % c
    incl = row >= col
    strict = row > col
    same_block = (row // RWKV_SOLVE_BLOCK) == (col // RWKV_SOLVE_BLOCK)
    eye_c = (row == col).astype(F32)
    lanes_of = lambda g: slice(g * gl, (g + 1) * gl)

    def blocks(x, w):
        x = x.astype(BF16)
        per_tile = max(LANES // w, 1)
        tile_w = per_tile * w
        n_tiles = x.shape[1] // tile_w
        in_tile = lax.broadcasted_iota(jnp.int32, (1, tile_w), 1) // w
        zeros = jnp.zeros((x.shape[0], tile_w), BF16)
        rows_ = []
        for h in range(hpb):
            piece = x[:, (h // per_tile) * tile_w:(h // per_tile + 1) * tile_w]
            if per_tile > 1:
                piece = piece * (in_tile == h % per_tile).astype(BF16)
            rows_.append(jnp.concatenate([piece if t == h // per_tile else zeros for t in range(n_tiles)], axis=1))
        return jnp.concatenate(rows_, axis=0)

    def mm_cc(x, y):
        return _dot(x.astype(BF16), blocks(y, c))

    def chunk_load(ci):
        rows = pl.ds(pl.multiple_of(ci * c, c), c)
        return tuple(s[rows, :] for s in (r_s, km_s, v_s, lw_s, kn_s, a_s, cs_s))

    def scaled_operands(loaded):
        rr, km, vv, lw, kn_c, aa, cs = loaded
        cs_last = cs[c - 1:c, :]
        e_neg = jnp.exp(-cs)
        e_end = jnp.exp(cs_last - cs)
        kb = kn_c * aa
        return dict(
            v=vv,
            rt=rr * jnp.exp(cs),
            at=-kn_c * jnp.exp(cs - lw),
            bt=kb * e_neg,
            kt=km * e_neg,
            bg=kb * e_end,
            kg=km * e_end,
            gam=jnp.exp(cs_last))

    def prepare(chunks):
        each = lambda f, *lists: [f(*args) for args in zip(*lists)]
        get = lambda name: [ch[name] for ch in chunks]
        v, rt, at, bt, kt, bg, kg = (get(n) for n in ("v", "rt", "at", "bt", "kt", "bg", "kg"))
        gram = each(lambda a_, r_, b_, k_: _dot_nt(
            jnp.concatenate([a_, r_], axis=0).astype(BF16),
            jnp.concatenate([blocks(b_, head), blocks(k_, head)], axis=0)), at, rt, bt, kt)
        blk = gl
        a_ab = [jnp.where(strict, g_[:c, :wide], 0.0) for g_ in gram]
        a_ak = [jnp.where(strict, g_[:c, wide:], 0.0) for g_ in gram]
        a_rb = [jnp.where(incl, g_[c:, :wide], 0.0) for g_ in gram]
        a_rk = [jnp.where(incl, g_[c:, wide:], 0.0) for g_ in gram]
        n_d = [jnp.where(same_block, x, 0.0) for x in a_ab]
        n_o = each(lambda x, d_: x - d_, a_ab, n_d)
        n_2 = each(mm_cc, n_d, n_d)
        n_4 = each(mm_cc, n_2, n_2)
        n_8 = each(mm_cc, n_4, n_4)
        t_lo = each(lambda x, y_: mm_cc(eye_c + x, eye_c + y_), n_d, n_2)
        t_hi = each(lambda x, y_: mm_cc(eye_c + x, eye_c + y_), n_4, n_8)
        t_d = each(mm_cc, t_lo, t_hi)
        x_1 = each(mm_cc, t_d, n_o)
        x_2 = each(mm_cc, x_1, x_1)
        t_mid = each(lambda x, t_: mm_cc(eye_c + x, t_), x_2, t_d)
        t_inv = each(lambda x, t_: mm_cc(eye_c + x, t_), x_1, t_mid)
        v_blocks = [blocks(x, head) for x in v]
        w_0 = each(lambda x, vb: _dot(x.astype(BF16), vb), a_ak, v_blocks)
        pq = each(lambda t_, a_, w_: _dot(t_.astype(BF16), jnp.concatenate(
            [blocks(a_, head), blocks(w_, head)], axis=1)), t_inv, at, w_0)
        rb_pq = each(lambda x, y_: _dot(x.astype(BF16), jnp.concatenate(
            [blocks(y_[:, :blk], head), blocks(y_[:, blk:], head)], axis=1)), a_rb, pq)
        rk_v = each(lambda x, vb: _dot(x.astype(BF16), vb), a_rk, v_blocks)
        bg_pq = each(_mm_tn, bg, pq)
        kg_v = each(_mm_tn, kg, v)
        r_eff = each(lambda r_, x: r_ + x[:, :blk], rt, rb_pq)
        y_in = each(lambda x, y_: x[:, blk:] + y_, rb_pq, rk_v)
        m_st = [jnp.where(eye_blk, jnp.broadcast_to(ch["gam"], (gl, gl)), 0.0)
                + jnp.where(same_head, x[:, :blk], 0.0) for ch, x in zip(chunks, bg_pq)]
        z_st = [jnp.where(same_head, x[:, blk:] + y_, 0.0) for x, y_ in zip(bg_pq, kg_v)]
        return r_eff, y_in, m_st, z_st

    def group_step(gi, carry):
        ids = [gi * unroll + u for u in range(unroll)]
        full = [scaled_operands(chunk_load(ci)) for ci in ids]
        items = [{name: arr[:, lanes_of(g)] for name, arr in ops.items()}
                 for ops in full for g in range(n_groups)]
        r_eff, y_in, m_st, z_st = prepare(items)
        for u, ci in enumerate(ids):
            rows = pl.ds(pl.multiple_of(ci * c, c), c)
            for g in range(n_groups):
                rp_s[rows, lanes_of(g)] = r_eff[u * n_groups + g]
                yi_s[rows, lanes_of(g)] = y_in[u * n_groups + g]
                m_s[ci * n_groups + g] = m_st[u * n_groups + g]
                z_s[ci * n_groups + g] = z_st[u * n_groups + g]
        return carry

    lax.fori_loop(0, n_chunks // unroll, group_step, 0)

    states = [state_ref[q] for q in range(n_groups)]
    y_chunks = []
    for ci in range(n_chunks):
        rp = rp_s[ci * c:(ci + 1) * c, :]
        y_parts = []
        for q in range(n_groups):
            y_parts.append(_mm(rp[:, lanes_of(q)], states[q]))
            states[q] = _mm(m_s[ci * n_groups + q], states[q]) + z_s[ci * n_groups + q]
        y_chunks.append(jnp.concatenate(y_parts, axis=1))
    for q in range(n_groups):
        state_ref[q] = states[q]
    y = jnp.concatenate(y_chunks, axis=0) + yi_s[...]

    inv_head = 1.0 / head
    mean = head_sum(y) * inv_head
    yc = y - mean
    var = head_sum(yc * yc) * inv_head
    yn = yc * lax.rsqrt(var + RWKV_LN_EPS)
    bonus = head_sum(r * k_mod * rk_ref[...]) * v
    o_ref[...] = ((yn * lnw_ref[...] + lnb_ref[...] + bonus) * gate).astype(o_ref.dtype)


def _rwkv_branch(p_rkv, p_small, lora_off, bsz, t_len, mu, w0, w_up, a0, a_up, g_up,
                 k_k, k_a, r_k, ln_w, ln_b):
    width = w0.shape[-1]
    head = r_k.shape[-1]
    wl, al, gl = w_up.shape[0], a_up.shape[0], g_up.shape[0]
    wp, ap, gp = (_round_up(n, LANES) for n in (wl, al, gl))
    off_g, off_w, off_a = lora_off
    blk = _pick(width, (RWKV_BLOCK_LANES, RWKV_GROUP_LANES, LANES))
    group = min(blk, RWKV_GROUP_LANES)
    n_groups = blk // group
    n_blk = width // blk
    tt = _pick(t_len, (512, 256, 128, 64))
    chunk = min(RWKV_CHUNK, tt)
    n_chunks = tt // chunk
    unroll = _pick(n_chunks, (RWKV_UNROLL, 2, 1))
    nt = t_len // tt
    mu_r, mu_k, mu_v = (mu[i * width:(i + 1) * width].reshape(1, width) for i in range(3))
    o = 3 * width
    mu_w = _pad_to(mu[o:o + wl], 0, wp).reshape(1, wp)
    mu_a = _pad_to(mu[o + wl:o + wl + al], 0, ap).reshape(1, ap)
    mu_g = _pad_to(mu[o + wl + al:o + wl + al + gl], 0, gp).reshape(1, gp)
    w_up_p = _pad_to(w_up, 0, wp).astype(BF16)
    a_up_p = _pad_to(a_up, 0, ap).astype(BF16)
    g_up_p = _pad_to(g_up, 0, gp).astype(BF16)
    vec = lambda a: a.reshape(1, width).astype(F32)

    w_tanh, a_in, g_sig = _lora_inputs(p_small, (off_w, off_a, off_g), (wp, ap, gp),
                                       (mu_w, mu_a, mu_g), bsz, t_len)

    row_blk = lambda b, p, t: b * nt + t
    col_spec = lambda j: pl.BlockSpec((tt, blk), lambda b, p, t, j=j: (row_blk(b, p, t), j * n_blk + p))
    lora_spec = lambda w_: pl.BlockSpec((tt, w_), lambda b, p, t: (row_blk(b, p, t), 0))
    chan_spec = pl.BlockSpec((1, blk), lambda b, p, t: (0, p))
    up_spec = lambda rows: pl.BlockSpec((rows, blk), lambda b, p, t: (0, p))

    return pl.pallas_call(
        functools.partial(_rwkv_kernel, head=head, chunk=chunk, unroll=unroll, group=group),
        grid=(bsz, n_blk, nt),
        in_specs=[col_spec(0), col_spec(1), col_spec(2),
                  lora_spec(wp), lora_spec(ap), lora_spec(gp),
                  chan_spec, chan_spec, chan_spec,
                  chan_spec, up_spec(wp), chan_spec, up_spec(ap), up_spec(gp),
                  chan_spec, chan_spec, chan_spec, chan_spec, chan_spec],
        out_specs=pl.BlockSpec((tt, blk), lambda b, p, t: (row_blk(b, p, t), p)),
        out_shape=jax.ShapeDtypeStruct((bsz * t_len, width), BF16),
        scratch_shapes=[pltpu.VMEM((SUBLANES, blk), F32)] * 3
                       + [pltpu.VMEM((tt, blk), F32)] * 9
                       + [pltpu.VMEM((n_chunks * n_groups, group, group), F32)] * 2
                       + [pltpu.VMEM((n_groups, group, group), F32)],
        compiler_params=_params(("parallel", "parallel", "arbitrary")),
        name="rwkv7_mixer",
    )(p_rkv, p_rkv, p_rkv, w_tanh, a_in, g_sig,
      mu_r, mu_k, mu_v,
      vec(w0), w_up_p, vec(a0), a_up_p, g_up_p,
      vec(k_k), vec(k_a), vec(r_k), vec(ln_w), vec(ln_b))


def _dt_kernel(dt_ref, bias_ref, alog_ref, dt_out_ref, acs_out_ref):
    n = dt_ref.shape[0]
    dt = _softplus(dt_ref[...] + bias_ref[...])
    a = dt * (-jnp.exp(alog_ref[...]))
    row = lax.broadcasted_iota(jnp.int32, (n, n), 0)
    col = lax.broadcasted_iota(jnp.int32, (n, n), 1)
    dt_out_ref[...] = dt
    acs_out_ref[...] = _dot((row >= col).astype(F32), a, precision=HIGHEST)


def _dt_tables(p_small, dt_off, hp, dt_bias, a_log):
    m = p_small.shape[0]
    spec = pl.BlockSpec((SSM_CHUNK, hp), lambda i: (i, 0))
    const = pl.BlockSpec((1, hp), lambda i: (0, 0))
    assert dt_off % hp == 0
    return pl.pallas_call(
        _dt_kernel,
        grid=(m // SSM_CHUNK,),
        in_specs=[pl.BlockSpec((SSM_CHUNK, hp), lambda i, o=dt_off // hp: (i, o)), const, const],
        out_specs=[spec, spec],
        out_shape=[jax.ShapeDtypeStruct((m, hp), F32)] * 2,
        compiler_params=_params(("parallel",)),
        name="ssd_dt_tables",
    )(p_small, _pad_to(dt_bias, 0, hp).reshape(1, hp).astype(F32),
      _pad_to(a_log, 0, hp).reshape(1, hp).astype(F32))


def _causal_conv(x, hist_ref, w_ref, b_ref):
    n = x.shape[0]
    k_len = w_ref.shape[0]
    hist_ref[SUBLANES:, :] = x
    acc = x * w_ref[k_len - 1:k_len, :] + b_ref[...]
    for shift in range(1, k_len):
        acc = acc + hist_ref[SUBLANES - shift:SUBLANES - shift + n, :] * w_ref[k_len - 1 - shift:k_len - shift, :]
    hist_ref[:SUBLANES, :] = x[n - SUBLANES:, :]
    return acc


def _ssd_kernel(z_ref, x_ref, b_ref, c_ref, dt_ref, acs_ref, acst_ref,
                cwx_ref, cwb_ref, cwc_ref, cbx_ref, cbb_ref, cbc_ref, dskip_ref, nw_ref,
                o_ref,
                tailx_ref, tailb_ref, tailc_ref, state_ref, *, headdim):
    n = x_ref.shape[0]
    gw = x_ref.shape[1]
    hpg = gw // headdim

    @pl.when(pl.program_id(2) == 0)
    def _():
        for ref in (tailx_ref, tailb_ref, tailc_ref, state_ref):
            ref[...] = jnp.zeros(ref.shape, ref.dtype)

    xg = _silu(_causal_conv(x_ref[...], tailx_ref, cwx_ref, cbx_ref))
    bm = _silu(_causal_conv(b_ref[...], tailb_ref, cwb_ref, cbb_ref))
    cm = _silu(_causal_conv(c_ref[...], tailc_ref, cwc_ref, cbc_ref))
    dt = dt_ref[0, 0]
    acs = acs_ref[0, 0]
    acs_t = acst_ref[0, 0]

    expand = (lax.broadcasted_iota(jnp.int32, (hpg, gw), 0)
              == lax.broadcasted_iota(jnp.int32, (hpg, gw), 1) // headdim).astype(BF16)

    def on_lanes(t):
        hi = t.astype(BF16)
        lo = (t - hi.astype(F32)).astype(BF16)
        return _dot(hi, expand) + _dot(lo, expand)

    dt_x = on_lanes(dt)
    decay_in_x = on_lanes(jnp.exp(acs))
    decay_out_x = on_lanes(jnp.exp(acs[n - 1:n, :] - acs))
    xd = xg * dt_x
    bm16 = bm.astype(BF16)
    cm16 = cm.astype(BF16)
    cb = _dot_nt(cm16, bm16)
    state = state_ref[...]
    y = _dot(cm16, state.astype(BF16)) * decay_in_x

    row = lax.broadcasted_iota(jnp.int32, (n, n), 0)
    col = lax.broadcasted_iota(jnp.int32, (n, n), 1)
    causal = row >= col
    parts = []
    for e in range(hpg):
        seg = jnp.where(causal, jnp.exp(acs[:, e:e + 1] - acs_t[e:e + 1, :]), 0.0)
        parts.append(_dot((cb * seg).astype(BF16), xd[:, e * headdim:(e + 1) * headdim].astype(BF16)))
    y = y + jnp.concatenate(parts, axis=1) + xg * dskip_ref[...]

    state_ref[...] = state * decay_in_x[n - 1:n, :] + _dot_tn(bm16, (xd * decay_out_x).astype(BF16))

    y = y * _silu(z_ref[...])
    y = y * lax.rsqrt(jnp.mean(y * y, axis=-1, keepdims=True) + RMS_EPS)
    o_ref[...] = (y * nw_ref[...]).astype(o_ref.dtype)


def _ssd_branch(p_zx, p_small, bc_off, dt_sp, acs, bsz, t_len, conv_w, conv_b, d_skip, norm_w):
    inner = norm_w.shape[-1]
    heads = d_skip.shape[-1]
    headdim = inner // heads
    groups, n_state = SSM_GROUPS, SSM_STATE
    gw = inner // groups
    hpg = heads // groups
    n = SSM_CHUNK
    nt = t_len // n
    off_b, off_c = bc_off
    assert off_b % n_state == 0 and off_c % n_state == 0
    k_len = conv_w.shape[0]
    gn = groups * n_state
    cw_x, cw_b, cw_c = conv_w[:, :inner], conv_w[:, inner:inner + gn], conv_w[:, inner + gn:]
    cb_x, cb_b, cb_c = (conv_b[s].reshape(1, -1) for s in
                        (slice(0, inner), slice(inner, inner + gn), slice(inner + gn, inner + 2 * gn)))
    per_group = lambda t: t[:, :heads].reshape(bsz, t_len, groups, hpg).transpose(0, 2, 1, 3)
    dt_g = per_group(dt_sp)
    acs_g = per_group(acs)
    acs_gt = acs_g.transpose(0, 1, 3, 2)
    d_exp = jnp.repeat(d_skip.astype(F32), headdim).reshape(1, inner)

    rows = lambda b, g, t: b * nt + t
    tab_spec = pl.BlockSpec((1, 1, n, hpg), lambda b, g, t: (b, g, t, 0))
    return pl.pallas_call(
        functools.partial(_ssd_kernel, headdim=headdim),
        grid=(bsz, groups, nt),
        in_specs=[pl.BlockSpec((n, gw), lambda b, g, t: (rows(b, g, t), g)),
                  pl.BlockSpec((n, gw), lambda b, g, t: (rows(b, g, t), groups + g)),
                  pl.BlockSpec((n, n_state), lambda b, g, t, o=off_b // n_state: (rows(b, g, t), o + g)),
                  pl.BlockSpec((n, n_state), lambda b, g, t, o=off_c // n_state: (rows(b, g, t), o + g)),
                  tab_spec, tab_spec,
                  pl.BlockSpec((1, 1, hpg, n), lambda b, g, t: (b, g, 0, t)),
                  pl.BlockSpec((k_len, gw), lambda b, g, t: (0, g)),
                  pl.BlockSpec((k_len, n_state), lambda b, g, t: (0, g)),
                  pl.BlockSpec((k_len, n_state), lambda b, g, t: (0, g)),
                  pl.BlockSpec((1, gw), lambda b, g, t: (0, g)),
                  pl.BlockSpec((1, n_state), lambda b, g, t: (0, g)),
                  pl.BlockSpec((1, n_state), lambda b, g, t: (0, g)),
                  pl.BlockSpec((1, gw), lambda b, g, t: (0, g)),
                  pl.BlockSpec((1, gw), lambda b, g, t: (0, g))],
        out_specs=pl.BlockSpec((n, gw), lambda b, g, t: (rows(b, g, t), g)),
        out_shape=jax.ShapeDtypeStruct((bsz * t_len, inner), BF16),
        scratch_shapes=[pltpu.VMEM((SUBLANES + n, gw), F32), pltpu.VMEM((SUBLANES + n, n_state), F32),
                        pltpu.VMEM((SUBLANES + n, n_state), F32), pltpu.VMEM((n_state, gw), F32)],
        compiler_params=_params(("parallel", "parallel", "arbitrary")),
        name="mamba2_ssd",
    )(p_zx, p_zx, p_small, p_small, dt_g, acs_g, acs_gt,
      cw_x, cw_b, cw_c, cb_x, cb_b, cb_c, d_exp, norm_w.reshape(1, inner).astype(F32))


def _gated(acc, gate):
    return _sigmoid(gate) * acc


def _gated_add(acc, gate, other):
    return other + _sigmoid(gate) * acc


def _residual(acc, res):
    return res + acc


def _relu_sq(acc):
    return jnp.square(jnp.maximum(acc, 0.0))


def kernel(x, norm_mix_w, w_in, rwkv_mu, rwkv_w0, rwkv_w_up, rwkv_a0, rwkv_a_up, rwkv_g_up, rwkv_k_k, rwkv_k_a, rwkv_r_k, rwkv_ln_w, rwkv_ln_b, ssm_conv_w, ssm_conv_b, ssm_dt_bias, ssm_a_log, ssm_d, ssm_norm_w, w_br_rwkv, w_br_ssm, w_out, norm_mlp_w, w_mlp_up, w_mlp_down, norm_final_w):
    bsz, t_len, d = x.shape
    m = bsz * t_len
    depth = w_in.shape[0]
    width = rwkv_w0.shape[-1]
    wl, al, gl = rwkv_w_up.shape[1], rwkv_a_up.shape[1], rwkv_g_up.shape[1]
    wp, ap, gp = (_round_up(n, LANES) for n in (wl, al, gl))
    rwkv_cols = 3 * width + wl + al + gl
    inner = ssm_norm_w.shape[-1]
    heads = ssm_a_log.shape[-1]
    hp = _round_up(heads, LANES)
    gn = SSM_GROUPS * SSM_STATE

    h = x.reshape(m, d).astype(F32)
    for layer in range(depth):
        o_z = rwkv_cols
        o_wd = 3 * width
        o_ad = o_wd + wl
        o_gd = o_ad + al
        w_t = jnp.transpose(w_in[layer])
        w_rkv = _transposed_cast(w_t, 0, 3 * width)
        n_tail = w_t.shape[0] - o_z
        n_main = n_tail // LANES * LANES
        tail = _transposed_cast(w_t, o_z, n_main)
        tail_end = jnp.transpose(lax.optimization_barrier(w_t[o_z + n_main:])).astype(BF16)
        lora_t = lax.optimization_barrier(w_t[o_wd:o_z])
        lora = lambda lo, hi, padded: _pad_to(jnp.transpose(lora_t[lo - o_wd:hi - o_wd]).astype(BF16), 1, padded)

        def tail_cols(lo, hi):
            if hi <= n_main:
                return tail[:, lo:hi]
            return jnp.concatenate([tail[:, lo:n_main], tail_end[:, :hi - n_main]], axis=1)

        t_bc = 2 * inner
        t_dt = t_bc + 2 * gn
        t_ga = t_dt + heads
        assert t_bc <= n_main
        w_gate = tail_cols(t_ga, t_ga + 2 * d)
        small = [tail_cols(t_bc, t_dt),
                 lora(o_gd, o_gd + gl, gp),
                 _pad_to(tail_cols(t_dt, t_ga), 1, hp),
                 lora(o_wd, o_wd + wl, wp),
                 lora(o_ad, o_ad + al, ap)]
        off_g = 2 * gn
        off_dt = off_g + gp
        off_w = off_dt + hp
        off_a = off_w + wp
        n_small = _round_up(off_a + ap, 1024)
        w_small = _pad_to(jnp.concatenate(small, axis=1), 1, n_small)

        u = _rmsnorm(h, norm_mix_w[layer], BF16)
        p_rkv = _matmul(u, w_rkv, F32, name="proj_rkv")
        p_zx = _matmul(u, tail, F32, n_cols=2 * inner, name="proj_zx")
        p_gate = _matmul(u, w_gate, F32, name="proj_gate")
        p_small = _matmul(u, w_small, F32, name="proj_small")

        y_a = _rwkv_branch(p_rkv, p_small, (off_g, off_w, off_a), bsz, t_len,
                           rwkv_mu[layer], rwkv_w0[layer], rwkv_w_up[layer], rwkv_a0[layer],
                           rwkv_a_up[layer], rwkv_g_up[layer], rwkv_k_k[layer], rwkv_k_a[layer],
                           rwkv_r_k[layer], rwkv_ln_w[layer], rwkv_ln_b[layer])

        dt_sp, acs = _dt_tables(p_small, off_dt, hp, ssm_dt_bias[layer], ssm_a_log[layer])
        y_b = _ssd_branch(p_zx, p_small, (0, gn), dt_sp, acs, bsz, t_len,
                          ssm_conv_w[layer], ssm_conv_b[layer], ssm_d[layer], ssm_norm_w[layer])

        part = _matmul(y_a, w_br_rwkv[layer], F32, epilogue=_gated,
                       extras=((p_gate, 0),), name="branch_rwkv")
        merged = _matmul(y_b, w_br_ssm[layer].astype(BF16), BF16, epilogue=_gated_add,
                         extras=((p_gate, d), (part, 0)), name="branch_ssm")
        h = _matmul(merged, w_out[layer], F32, epilogue=_residual,
                    extras=((h, 0),), name="out_proj")

        u = _rmsnorm(h, norm_mlp_w[layer], BF16)
        hidden = _matmul(u, w_mlp_up[layer], BF16, epilogue=_relu_sq,
                         name="mlp_up")
        h = _matmul(hidden, w_mlp_down[layer].astype(BF16), F32, epilogue=_residual,
                    extras=((h, 0),), name="mlp_down")
    return _rmsnorm(h, norm_final_w, x.dtype).reshape(bsz, t_len, d)
```

```python
import functools

import jax
import jax.numpy as jnp
from jax import lax
from jax.experimental import pallas as pl
from jax.experimental.pallas import tpu as pltpu

F32 = jnp.float32
BF16 = jnp.bfloat16
HIGHEST = lax.Precision.HIGHEST

LANES = 128
SUBLANES = 8
VMEM_LIMIT_BYTES = 56 * 1024 * 1024
MATMUL_VMEM_BUDGET = 44 * 1024 * 1024
MATMUL_MIN_REUSE = 256

RMS_EPS = 1e-5
RWKV_LN_EPS = 64e-5
RWKV_CHUNK = 64
RWKV_SOLVE_BLOCK = 16
RWKV_BLOCK_LANES = 1024
RWKV_GROUP_LANES = 256
RWKV_UNROLL = 8
SSM_GROUPS = 8
SSM_STATE = 128
SSM_CHUNK = 128


def _round_up(n, m):
    return (n + m - 1) // m * m


def _pick(dim, prefs):
    for p in prefs:
        if dim % p == 0:
            return p
    return dim


def _pad_to(a, axis, size):
    if a.shape[axis] == size:
        return a
    pad = [(0, 0)] * a.ndim
    pad[axis] = (0, size - a.shape[axis])
    return jnp.pad(a, pad)


def _params(sem):
    return pltpu.CompilerParams(dimension_semantics=sem, vmem_limit_bytes=VMEM_LIMIT_BYTES)


def _dot(a, b, precision=None):
    return jnp.dot(a, b, preferred_element_type=F32, precision=precision)


def _dot_nt(a, b, precision=None):
    return lax.dot_general(a, b, (((1,), (1,)), ((), ())), preferred_element_type=F32,
                           precision=precision)


def _dot_tn(a, b, precision=None):
    return lax.dot_general(a, b, (((0,), (0,)), ((), ())), preferred_element_type=F32,
                           precision=precision)


def _sigmoid(x):
    return 1.0 / (1.0 + jnp.exp(-x))


def _silu(x):
    return x * _sigmoid(x)


def _softplus(x):
    return jnp.maximum(x, 0.0) + jnp.log(1.0 + jnp.exp(-jnp.abs(x)))


def _rmsnorm_kernel(x_ref, w_ref, o_ref):
    x = x_ref[...]
    y = x * lax.rsqrt(jnp.mean(x * x, axis=-1, keepdims=True) + RMS_EPS)
    o_ref[...] = (y * w_ref[...]).astype(o_ref.dtype)


def _rmsnorm(x2d, w, out_dtype):
    m, d = x2d.shape
    tr = _pick(m, (256, 128, 64, 32, 16, 8))
    return pl.pallas_call(
        _rmsnorm_kernel,
        grid=(m // tr,),
        in_specs=[pl.BlockSpec((tr, d), lambda i: (i, 0)),
                  pl.BlockSpec((1, d), lambda i: (0, 0))],
        out_specs=pl.BlockSpec((tr, d), lambda i: (i, 0)),
        out_shape=jax.ShapeDtypeStruct((m, d), out_dtype),
        compiler_params=_params(("parallel",)),
        name="rmsnorm",
    )(x2d, w.reshape(1, d).astype(F32))


def _mm_kernel(*refs, n_extra, epilogue, nk):
    a_ref, w_ref = refs[0], refs[1]
    extras = refs[2:2 + n_extra]
    o_ref = refs[2 + n_extra]
    part = _dot(a_ref[...], w_ref[...].astype(BF16))
    if nk == 1:
        o_ref[...] = epilogue(part, *[e[...] for e in extras]).astype(o_ref.dtype)
        return
    acc_ref = refs[3 + n_extra]
    k = pl.program_id(2)

    @pl.when(k == 0)
    def _():
        acc_ref[...] = part

    @pl.when(k > 0)
    def _():
        acc_ref[...] += part

    @pl.when(k == nk - 1)
    def _():
        o_ref[...] = epilogue(acc_ref[...], *[e[...] for e in extras]).astype(o_ref.dtype)


def _matmul_tiles(m, k, n, w_bytes, out_bytes, extras):
    best = None
    for tk in (k, k // 2, k // 4, k // 8):
        if tk < 1 or k % tk or (tk != k and tk % LANES):
            continue
        for tm in (1024, 512, 256, 128, 64, 32, 16, 8):
            for tn in (1024, 512, 256, 128):
                if m % tm or n % tn or any(off % tn for _, off in extras):
                    continue
                vmem = 2 * tk * (2 * tm + w_bytes * tn) + 2 * tm * tn * (out_bytes + 4 * len(extras))
                vmem += 4 * tm * tn if tk != k else 0
                vmem += 2 * tk * tn if w_bytes != 2 else 0
                if vmem > MATMUL_VMEM_BUDGET:
                    continue
                reuse = tm * tn / (tm + tn)
                score = (reuse >= MATMUL_MIN_REUSE, tk == k, reuse, tk, tm)
                if best is None or score > best[0]:
                    best = (score, (tm, tn, tk))
    assert best is not None, (m, k, n)
    return best[1]


def _matmul(a, w, out_dtype, epilogue=None, extras=(), n_cols=None, name="matmul"):
    m, k = a.shape
    n = w.shape[1] if n_cols is None else n_cols
    if epilogue is None:
        epilogue = lambda acc: acc
    tm, tn, tk = _matmul_tiles(m, k, n, w.dtype.itemsize, jnp.dtype(out_dtype).itemsize, extras)
    nk = k // tk
    in_specs = [pl.BlockSpec((tm, tk), lambda i, j, kk: (i, kk)),
                pl.BlockSpec((tk, tn), lambda i, j, kk: (kk, j))]
    operands = [a, w]
    for arr, off in extras:
        assert off % tn == 0, (off, tn)
        in_specs.append(pl.BlockSpec((tm, tn), lambda i, j, kk, o=off // tn: (i, j + o)))
        operands.append(arr)
    return pl.pallas_call(
        functools.partial(_mm_kernel, n_extra=len(extras), epilogue=epilogue, nk=nk),
        grid=(m // tm, n // tn, nk),
        in_specs=in_specs,
        out_specs=pl.BlockSpec((tm, tn), lambda i, j, kk: (i, j)),
        out_shape=jax.ShapeDtypeStruct((m, n), out_dtype),
        scratch_shapes=[pltpu.VMEM((tm, tn), F32)] if nk > 1 else [],
        compiler_params=_params(("parallel", "parallel", "arbitrary")),
        name=name,
    )(*operands)


def _transpose_kernel(wt_hbm, o_ref, buf_ref, sem_ref, *, row0, n_col_blocks, n_steps):
    rows, wb = o_ref.shape
    step = pl.program_id(0) * n_col_blocks + pl.program_id(1)

    def window(s, slot):
        r0 = pl.multiple_of(row0 + (s % n_col_blocks) * wb, SUBLANES)
        c0 = pl.multiple_of((s // n_col_blocks) * rows, LANES)
        return pltpu.make_async_copy(wt_hbm.at[pl.ds(r0, wb), pl.ds(c0, rows)],
                                     buf_ref.at[slot], sem_ref.at[slot])

    @pl.when(step == 0)
    def _():
        window(step, 0).start()

    @pl.when(step + 1 < n_steps)
    def _():
        window(step + 1, (step + 1) % 2).start()

    window(step, step % 2).wait()
    o_ref[...] = buf_ref[step % 2].T.astype(o_ref.dtype)


def _transposed_cast(wt, row_start, n_rows):
    k = wt.shape[1]
    assert row_start % SUBLANES == 0 and n_rows % LANES == 0 and k % LANES == 0
    wb = next(c for c in range(2048, 0, -LANES) if n_rows % c == 0)
    rows = _pick(k, (512, 256, 128))
    n_r, n_c = k // rows, n_rows // wb
    return pl.pallas_call(
        functools.partial(_transpose_kernel, row0=row_start, n_col_blocks=n_c, n_steps=n_r * n_c),
        grid=(n_r, n_c),
        in_specs=[pl.BlockSpec(memory_space=pl.ANY)],
        out_specs=pl.BlockSpec((rows, wb), lambda i, j: (i, j)),
        out_shape=jax.ShapeDtypeStruct((k, n_rows), BF16),
        scratch_shapes=[pltpu.VMEM((2, wb, rows), wt.dtype), pltpu.SemaphoreType.DMA((2,))],
        compiler_params=_params(("arbitrary", "arbitrary")),
        name="transposed_cast",
    )(wt)


def _token_shift_mix(x, carry_ref, mu):
    rolled = pltpu.roll(x, 1, 0)
    first = lax.broadcasted_iota(jnp.int32, (SUBLANES, x.shape[1]), 0) == 0
    top = jnp.where(first, carry_ref[SUBLANES - 1:SUBLANES, :], rolled[:SUBLANES, :])
    prev = jnp.concatenate([top, rolled[SUBLANES:, :]], axis=0)
    carry_ref[...] = x[x.shape[0] - SUBLANES:, :]
    return x + (prev - x) * mu


def _chunk_cumsum(x, chunk):
    rows = lax.broadcasted_iota(jnp.int32, x.shape, 0) & (chunk - 1)
    shift = 1
    while shift < chunk:
        x = x + jnp.where(rows >= shift, pltpu.roll(x, shift, 0), 0.0)
        shift *= 2
    return x


def _mm(a, b):
    return _dot(a.astype(BF16), b.astype(BF16))


def _mm_nt(a, b):
    return _dot_nt(a.astype(BF16), b.astype(BF16))


def _mm_tn(a, b):
    return _dot_tn(a.astype(BF16), b.astype(BF16))


def _lora_kernel(wd_ref, ad_ref, gd_ref, muw_ref, mua_ref, mug_ref, wt_ref, at_ref, gs_ref,
                 cw_ref, ca_ref, cg_ref):
    @pl.when(pl.program_id(1) == 0)
    def _():
        for ref in (cw_ref, ca_ref, cg_ref):
            ref[...] = jnp.zeros(ref.shape, ref.dtype)

    wt_ref[...] = jnp.tanh(_token_shift_mix(wd_ref[...], cw_ref, muw_ref[...])).astype(wt_ref.dtype)
    at_ref[...] = _token_shift_mix(ad_ref[...], ca_ref, mua_ref[...]).astype(at_ref.dtype)
    gs_ref[...] = _sigmoid(_token_shift_mix(gd_ref[...], cg_ref, mug_ref[...])).astype(gs_ref.dtype)


def _lora_inputs(p_small, offs, widths, mus, bsz, t_len):
    m = p_small.shape[0]
    tt = _pick(t_len, (512, 256, 128, 64))
    nt = t_len // tt
    for off, w_ in zip(offs, widths):
        assert off % w_ == 0, (off, w_)
    in_spec = lambda w_, off: pl.BlockSpec((tt, w_), lambda b, t, o_=off // w_: (b * nt + t, o_))
    out_spec = lambda w_: pl.BlockSpec((tt, w_), lambda b, t: (b * nt + t, 0))
    const = lambda w_: pl.BlockSpec((1, w_), lambda b, t: (0, 0))
    return pl.pallas_call(
        _lora_kernel,
        grid=(bsz, nt),
        in_specs=[in_spec(w_, off) for w_, off in zip(widths, offs)] + [const(w_) for w_ in widths],
        out_specs=[out_spec(w_) for w_ in widths],
        out_shape=[jax.ShapeDtypeStruct((m, w_), BF16) for w_ in widths],
        scratch_shapes=[pltpu.VMEM((SUBLANES, w_), F32) for w_ in widths],
        compiler_params=_params(("parallel", "arbitrary")),
        name="rwkv7_lora_inputs",
    )(p_small, p_small, p_small, *mus)


def _rwkv_kernel(r_ref, k_ref, v_ref, wt_ref, at_ref, gs_ref,
                 mur_ref, muk_ref, muv_ref,
                 w0_ref, wup_ref, a0_ref, aup_ref, gup_ref,
                 kk_ref, ka_ref, rk_ref, lnw_ref, lnb_ref,
                 o_ref,
                 cr_ref, ck_ref, cv_ref,
                 r_s, km_s, v_s, lw_s, kn_s, a_s, cs_s, rp_s, yi_s, m_s, z_s, state_ref,
                 *, head, chunk, unroll, group):
    tt, blk = r_ref.shape
    gl = group
    n_groups = blk // gl
    hpb = gl // head
    c = chunk
    n_chunks = tt // c

    @pl.when(pl.program_id(2) == 0)
    def _():
        for ref in (cr_ref, ck_ref, cv_ref, state_ref):
            ref[...] = jnp.zeros(ref.shape, ref.dtype)

    lane = lax.broadcasted_iota(jnp.int32, (1, gl), 1)
    brow = lax.broadcasted_iota(jnp.int32, (gl, gl), 0)
    bcol = lax.broadcasted_iota(jnp.int32, (gl, gl), 1)
    same_head = (brow // head) == (bcol // head)
    eye_blk = brow == bcol
    ones_heads = same_head.astype(BF16)

    def head_sum(x):
        x16 = x.astype(BF16)
        return jnp.concatenate([_dot(x16[:, q * gl:(q + 1) * gl], ones_heads) for q in range(n_groups)], axis=1)

    r = _token_shift_mix(r_ref[...], cr_ref, mur_ref[...])
    k = _token_shift_mix(k_ref[...], ck_ref, muk_ref[...])
    v = _token_shift_mix(v_ref[...], cv_ref, muv_ref[...])

    w_lin = w0_ref[...] + _dot(wt_ref[...], wup_ref[...])
    w = -_softplus(-w_lin) - 0.5
    log_decay = -jnp.exp(w)
    a = _sigmoid(a0_ref[...] + _dot(at_ref[...], aup_ref[...]))
    gate = _dot(gs_ref[...], gup_ref[...])
    k_mod = k * (1.0 + (a - 1.0) * ka_ref[...])
    kk_raw = k * kk_ref[...]
    kn = kk_raw / jnp.maximum(jnp.sqrt(head_sum(kk_raw * kk_raw)), 1e-12)

    r_s[...] = r
    km_s[...] = k_mod
    v_s[...] = v
    lw_s[...] = log_decay
    kn_s[...] = kn
    a_s[...] = a
    cs_s[...] = _chunk_cumsum(log_decay, c)

    wide = hpb * c
    row = lax.broadcasted_iota(jnp.int32, (c, wide), 0)
    col = lax.broadcasted_iota(jnp.int32, (c, wide), 1) % c
    incl = row >= col
    strict = row > col
    same_block = (row // RWKV_SOLVE_BLOCK) == (col // RWKV_SOLVE_BLOCK)
    eye_c = (row == col).astype(F32)
    lanes_of = lambda g: slice(g * gl, (g + 1) * gl)

    def blocks(x, w):
        x = x.astype(BF16)
        per_tile = max(LANES // w, 1)
        tile_w = per_tile * w
        n_tiles = x.shape[1] // tile_w
        in_tile = lax.broadcasted_iota(jnp.int32, (1, tile_w), 1) // w
        zeros = jnp.zeros((x.shape[0], tile_w), BF16)
        rows_ = []
        for h in range(hpb):
            piece = x[:, (h // per_tile) * tile_w:(h // per_tile + 1) * tile_w]
            if per_tile > 1:
                piece = piece * (in_tile == h % per_tile).astype(BF16)
            rows_.append(jnp.concatenate([piece if t == h // per_tile else zeros for t in range(n_tiles)], axis=1))
        return jnp.concatenate(rows_, axis=0)

    def mm_cc(x, y):
        return _dot(x.astype(BF16), blocks(y, c))

    def chunk_load(ci):
        rows = pl.ds(pl.multiple_of(ci * c, c), c)
        return tuple(s[rows, :] for s in (r_s, km_s, v_s, lw_s, kn_s, a_s, cs_s))

    def scaled_operands(loaded):
        rr, km, vv, lw, kn_c, aa, cs = loaded
        cs_last = cs[c - 1:c, :]
        e_neg = jnp.exp(-cs)
        e_end = jnp.exp(cs_last - cs)
        kb = kn_c * aa
        return dict(
            v=vv,
            rt=rr * jnp.exp(cs),
            at=-kn_c * jnp.exp(cs - lw),
            bt=kb * e_neg,
            kt=km * e_neg,
            bg=kb * e_end,
            kg=km * e_end,
            gam=jnp.exp(cs_last))

    def prepare(chunks):
        each = lambda f, *lists: [f(*args) for args in zip(*lists)]
        get = lambda name: [ch[name] for ch in chunks]
        v, rt, at, bt, kt, bg, kg = (get(n) for n in ("v", "rt", "at", "bt", "kt", "bg", "kg"))
        gram = each(lambda a_, r_, b_, k_: _dot_nt(
            jnp.concatenate([a_, r_], axis=0).astype(BF16),
            jnp.concatenate([blocks(b_, head), blocks(k_, head)], axis=0)), at, rt, bt, kt)
        blk = gl
        a_ab = [jnp.where(strict, g_[:c, :wide], 0.0) for g_ in gram]
        a_ak = [jnp.where(strict, g_[:c, wide:], 0.0) for g_ in gram]
        a_rb = [jnp.where(incl, g_[c:, :wide], 0.0) for g_ in gram]
        a_rk = [jnp.where(incl, g_[c:, wide:], 0.0) for g_ in gram]
        n_d = [jnp.where(same_block, x, 0.0) for x in a_ab]
        n_o = each(lambda x, d_: x - d_, a_ab, n_d)
        n_2 = each(mm_cc, n_d, n_d)
        n_4 = each(mm_cc, n_2, n_2)
        n_8 = each(mm_cc, n_4, n_4)
        t_lo = each(lambda x, y_: mm_cc(eye_c + x, eye_c + y_), n_d, n_2)
        t_hi = each(lambda x, y_: mm_cc(eye_c + x, eye_c + y_), n_4, n_8)
        t_d = each(mm_cc, t_lo, t_hi)
        x_1 = each(mm_cc, t_d, n_o)
        x_2 = each(mm_cc, x_1, x_1)
        t_mid = each(lambda x, t_: mm_cc(eye_c + x, t_), x_2, t_d)
        t_inv = each(lambda x, t_: mm_cc(eye_c + x, t_), x_1, t_mid)
        v_blocks = [blocks(x, head) for x in v]
        w_0 = each(lambda x, vb: _dot(x.astype(BF16), vb), a_ak, v_blocks)
        pq = each(lambda t_, a_, w_: _dot(t_.astype(BF16), jnp.concatenate(
            [blocks(a_, head), blocks(w_, head)], axis=1)), t_inv, at, w_0)
        rb_pq = each(lambda x, y_: _dot(x.astype(BF16), jnp.concatenate(
            [blocks(y_[:, :blk], head), blocks(y_[:, blk:], head)], axis=1)), a_rb, pq)
        rk_v = each(lambda x, vb: _dot(x.astype(BF16), vb), a_rk, v_blocks)
        no_p = jnp.zeros((c, blk), BF16)
        mz = each(lambda b_, k_, x, v_: _dot_tn(
            jnp.concatenate([b_, k_], axis=0).astype(BF16),
            jnp.concatenate([x.astype(BF16), jnp.concatenate([no_p, v_.astype(BF16)], axis=1)], axis=0)),
            bg, kg, pq, v)
        r_eff = each(lambda r_, x: r_ + x[:, :blk], rt, rb_pq)
        y_in = each(lambda x, y_: x[:, blk:] + y_, rb_pq, rk_v)
        m_st = [jnp.where(eye_blk, jnp.broadcast_to(ch["gam"], (gl, gl)), 0.0)
                + jnp.where(same_head, x[:, :blk], 0.0) for ch, x in zip(chunks, mz)]
        z_st = [jnp.where(same_head, x[:, blk:], 0.0) for x in mz]
        return r_eff, y_in, m_st, z_st

    def group_step(gi, carry):
        ids = [gi * unroll + u for u in range(unroll)]
        full = [scaled_operands(chunk_load(ci)) for ci in ids]
        items = [{name: arr[:, lanes_of(g)] for name, arr in ops.items()}
                 for ops in full for g in range(n_groups)]
        r_eff, y_in, m_st, z_st = prepare(items)
        for u, ci in enumerate(ids):
            rows = pl.ds(pl.multiple_of(ci * c, c), c)
            for g in range(n_groups):
                rp_s[rows, lanes_of(g)] = r_eff[u * n_groups + g]
                yi_s[rows, lanes_of(g)] = y_in[u * n_groups + g]
                m_s[ci * n_groups + g] = m_st[u * n_groups + g]
                z_s[ci * n_groups + g] = z_st[u * n_groups + g]
        return carry

    lax.fori_loop(0, n_chunks // unroll, group_step, 0)

    states = [state_ref[q] for q in range(n_groups)]
    y_chunks = []
    for ci in range(n_chunks):
        rp = rp_s[ci * c:(ci + 1) * c, :]
        y_parts = []
        for q in range(n_groups):
            y_parts.append(_mm(rp[:, lanes_of(q)], states[q]))
            states[q] = _mm(m_s[ci * n_groups + q], states[q]) + z_s[ci * n_groups + q]
        y_chunks.append(jnp.concatenate(y_parts, axis=1))
    for q in range(n_groups):
        state_ref[q] = states[q]
    y = jnp.concatenate(y_chunks, axis=0) + yi_s[...]

    inv_head = 1.0 / head
    mean = head_sum(y) * inv_head
    yc = y - mean
    var = head_sum(yc * yc) * inv_head
    yn = yc * lax.rsqrt(var + RWKV_LN_EPS)
    bonus = head_sum(r * k_mod * rk_ref[...]) * v
    o_ref[...] = ((yn * lnw_ref[...] + lnb_ref[...] + bonus) * gate).astype(o_ref.dtype)


def _rwkv_branch(p_rkv, p_small, lora_off, bsz, t_len, mu, w0, w_up, a0, a_up, g_up,
                 k_k, k_a, r_k, ln_w, ln_b):
    width = w0.shape[-1]
    head = r_k.shape[-1]
    wl, al, gl = w_up.shape[0], a_up.shape[0], g_up.shape[0]
    wp, ap, gp = (_round_up(n, LANES) for n in (wl, al, gl))
    off_g, off_w, off_a = lora_off
    blk = _pick(width, (RWKV_BLOCK_LANES, RWKV_GROUP_LANES, LANES))
    group = min(blk, RWKV_GROUP_LANES)
    n_groups = blk // group
    n_blk = width // blk
    tt = _pick(t_len, (512, 256, 128, 64))
    chunk = min(RWKV_CHUNK, tt)
    n_chunks = tt // chunk
    unroll = _pick(n_chunks, (RWKV_UNROLL, 2, 1))
    nt = t_len // tt
    mu_r, mu_k, mu_v = (mu[i * width:(i + 1) * width].reshape(1, width) for i in range(3))
    o = 3 * width
    mu_w = _pad_to(mu[o:o + wl], 0, wp).reshape(1, wp)
    mu_a = _pad_to(mu[o + wl:o + wl + al], 0, ap).reshape(1, ap)
    mu_g = _pad_to(mu[o + wl + al:o + wl + al + gl], 0, gp).reshape(1, gp)
    w_up_p = _pad_to(w_up, 0, wp).astype(BF16)
    a_up_p = _pad_to(a_up, 0, ap).astype(BF16)
    g_up_p = _pad_to(g_up, 0, gp).astype(BF16)
    vec = lambda a: a.reshape(1, width).astype(F32)

    w_tanh, a_in, g_sig = _lora_inputs(p_small, (off_w, off_a, off_g), (wp, ap, gp),
                                       (mu_w, mu_a, mu_g), bsz, t_len)

    row_blk = lambda b, p, t: b * nt + t
    col_spec = lambda j: pl.BlockSpec((tt, blk), lambda b, p, t, j=j: (row_blk(b, p, t), j * n_blk + p))
    lora_spec = lambda w_: pl.BlockSpec((tt, w_), lambda b, p, t: (row_blk(b, p, t), 0))
    chan_spec = pl.BlockSpec((1, blk), lambda b, p, t: (0, p))
    up_spec = lambda rows: pl.BlockSpec((rows, blk), lambda b, p, t: (0, p))

    return pl.pallas_call(
        functools.partial(_rwkv_kernel, head=head, chunk=chunk, unroll=unroll, group=group),
        grid=(bsz, n_blk, nt),
        in_specs=[col_spec(0), col_spec(1), col_spec(2),
                  lora_spec(wp), lora_spec(ap), lora_spec(gp),
                  chan_spec, chan_spec, chan_spec,
                  chan_spec, up_spec(wp), chan_spec, up_spec(ap), up_spec(gp),
                  chan_spec, chan_spec, chan_spec, chan_spec, chan_spec],
        out_specs=pl.BlockSpec((tt, blk), lambda b, p, t: (row_blk(b, p, t), p)),
        out_shape=jax.ShapeDtypeStruct((bsz * t_len, width), BF16),
        scratch_shapes=[pltpu.VMEM((SUBLANES, blk), F32)] * 3
                       + [pltpu.VMEM((tt, blk), F32)] * 9
                       + [pltpu.VMEM((n_chunks * n_groups, group, group), F32)] * 2
                       + [pltpu.VMEM((n_groups, group, group), F32)],
        compiler_params=_params(("parallel", "parallel", "arbitrary")),
        name="rwkv7_mixer",
    )(p_rkv, p_rkv, p_rkv, w_tanh, a_in, g_sig,
      mu_r, mu_k, mu_v,
      vec(w0), w_up_p, vec(a0), a_up_p, g_up_p,
      vec(k_k), vec(k_a), vec(r_k), vec(ln_w), vec(ln_b))


def _dt_kernel(dt_ref, bias_ref, alog_ref, dt_out_ref, acs_out_ref):
    n = dt_ref.shape[0]
    dt = _softplus(dt_ref[...] + bias_ref[...])
    a = dt * (-jnp.exp(alog_ref[...]))
    row = lax.broadcasted_iota(jnp.int32, (n, n), 0)
    col = lax.broadcasted_iota(jnp.int32, (n, n), 1)
    dt_out_ref[...] = dt
    acs_out_ref[...] = _dot((row >= col).astype(F32), a, precision=HIGHEST)


def _dt_tables(p_small, dt_off, hp, dt_bias, a_log):
    m = p_small.shape[0]
    spec = pl.BlockSpec((SSM_CHUNK, hp), lambda i: (i, 0))
    const = pl.BlockSpec((1, hp), lambda i: (0, 0))
    assert dt_off % hp == 0
    return pl.pallas_call(
        _dt_kernel,
        grid=(m // SSM_CHUNK,),
        in_specs=[pl.BlockSpec((SSM_CHUNK, hp), lambda i, o=dt_off // hp: (i, o)), const, const],
        out_specs=[spec, spec],
        out_shape=[jax.ShapeDtypeStruct((m, hp), F32)] * 2,
        compiler_params=_params(("parallel",)),
        name="ssd_dt_tables",
    )(p_small, _pad_to(dt_bias, 0, hp).reshape(1, hp).astype(F32),
      _pad_to(a_log, 0, hp).reshape(1, hp).astype(F32))


def _causal_conv(x, hist_ref, w_ref, b_ref):
    n = x.shape[0]
    k_len = w_ref.shape[0]
    hist_ref[SUBLANES:, :] = x
    acc = x * w_ref[k_len - 1:k_len, :] + b_ref[...]
    for shift in range(1, k_len):
        acc = acc + hist_ref[SUBLANES - shift:SUBLANES - shift + n, :] * w_ref[k_len - 1 - shift:k_len - shift, :]
    hist_ref[:SUBLANES, :] = x[n - SUBLANES:, :]
    return acc


def _ssd_kernel(z_ref, x_ref, b_ref, c_ref, dt_ref, acs_ref, acst_ref,
                cwx_ref, cwb_ref, cwc_ref, cbx_ref, cbb_ref, cbc_ref, dskip_ref, nw_ref,
                o_ref,
                tailx_ref, tailb_ref, tailc_ref, state_ref, *, headdim):
    n = x_ref.shape[0]
    gw = x_ref.shape[1]
    hpg = gw // headdim

    @pl.when(pl.program_id(2) == 0)
    def _():
        for ref in (tailx_ref, tailb_ref, tailc_ref, state_ref):
            ref[...] = jnp.zeros(ref.shape, ref.dtype)

    xg = _silu(_causal_conv(x_ref[...], tailx_ref, cwx_ref, cbx_ref))
    bm = _silu(_causal_conv(b_ref[...], tailb_ref, cwb_ref, cbb_ref))
    cm = _silu(_causal_conv(c_ref[...], tailc_ref, cwc_ref, cbc_ref))
    dt = dt_ref[0, 0]
    acs = acs_ref[0, 0]
    acs_t = acst_ref[0, 0]

    expand = (lax.broadcasted_iota(jnp.int32, (hpg, gw), 0)
              == lax.broadcasted_iota(jnp.int32, (hpg, gw), 1) // headdim).astype(BF16)

    def on_lanes(t):
        hi = t.astype(BF16)
        lo = (t - hi.astype(F32)).astype(BF16)
        return _dot(hi, expand) + _dot(lo, expand)

    dt_x = on_lanes(dt)
    decay_in_x = on_lanes(jnp.exp(acs))
    decay_out_x = on_lanes(jnp.exp(acs[n - 1:n, :] - acs))
    xd = xg * dt_x
    bm16 = bm.astype(BF16)
    cm16 = cm.astype(BF16)
    cb = _dot_nt(cm16, bm16)
    state = state_ref[...]
    y = _dot(cm16, state.astype(BF16)) * decay_in_x

    row = lax.broadcasted_iota(jnp.int32, (n, n), 0)
    col = lax.broadcasted_iota(jnp.int32, (n, n), 1)
    causal = row >= col
    parts = []
    for e in range(hpg):
        seg = jnp.where(causal, jnp.exp(acs[:, e:e + 1] - acs_t[e:e + 1, :]), 0.0)
        parts.append(_dot((cb * seg).astype(BF16), xd[:, e * headdim:(e + 1) * headdim].astype(BF16)))
    y = y + jnp.concatenate(parts, axis=1) + xg * dskip_ref[...]

    state_ref[...] = state * decay_in_x[n - 1:n, :] + _dot_tn(bm16, (xd * decay_out_x).astype(BF16))

    y = y * _silu(z_ref[...])
    y = y * lax.rsqrt(jnp.mean(y * y, axis=-1, keepdims=True) + RMS_EPS)
    o_ref[...] = (y * nw_ref[...]).astype(o_ref.dtype)


def _ssd_branch(p_zx, p_small, bc_off, dt_sp, acs, bsz, t_len, conv_w, conv_b, d_skip, norm_w):
    inner = norm_w.shape[-1]
    heads = d_skip.shape[-1]
    headdim = inner // heads
    groups, n_state = SSM_GROUPS, SSM_STATE
    gw = inner // groups
    hpg = heads // groups
    n = SSM_CHUNK
    nt = t_len // n
    off_b, off_c = bc_off
    assert off_b % n_state == 0 and off_c % n_state == 0
    k_len = conv_w.shape[0]
    gn = groups * n_state
    cw_x, cw_b, cw_c = conv_w[:, :inner], conv_w[:, inner:inner + gn], conv_w[:, inner + gn:]
    cb_x, cb_b, cb_c = (conv_b[s].reshape(1, -1) for s in
                        (slice(0, inner), slice(inner, inner + gn), slice(inner + gn, inner + 2 * gn)))
    per_group = lambda t: t[:, :heads].reshape(bsz, t_len, groups, hpg).transpose(0, 2, 1, 3)
    dt_g = per_group(dt_sp)
    acs_g = per_group(acs)
    acs_gt = acs_g.transpose(0, 1, 3, 2)
    d_exp = jnp.repeat(d_skip.astype(F32), headdim).reshape(1, inner)

    rows = lambda b, g, t: b * nt + t
    tab_spec = pl.BlockSpec((1, 1, n, hpg), lambda b, g, t: (b, g, t, 0))
    return pl.pallas_call(
        functools.partial(_ssd_kernel, headdim=headdim),
        grid=(bsz, groups, nt),
        in_specs=[pl.BlockSpec((n, gw), lambda b, g, t: (rows(b, g, t), g)),
                  pl.BlockSpec((n, gw), lambda b, g, t: (rows(b, g, t), groups + g)),
                  pl.BlockSpec((n, n_state), lambda b, g, t, o=off_b // n_state: (rows(b, g, t), o + g)),
                  pl.BlockSpec((n, n_state), lambda b, g, t, o=off_c // n_state: (rows(b, g, t), o + g)),
                  tab_spec, tab_spec,
                  pl.BlockSpec((1, 1, hpg, n), lambda b, g, t: (b, g, 0, t)),
                  pl.BlockSpec((k_len, gw), lambda b, g, t: (0, g)),
                  pl.BlockSpec((k_len, n_state), lambda b, g, t: (0, g)),
                  pl.BlockSpec((k_len, n_state), lambda b, g, t: (0, g)),
                  pl.BlockSpec((1, gw), lambda b, g, t: (0, g)),
                  pl.BlockSpec((1, n_state), lambda b, g, t: (0, g)),
                  pl.BlockSpec((1, n_state), lambda b, g, t: (0, g)),
                  pl.BlockSpec((1, gw), lambda b, g, t: (0, g)),
                  pl.BlockSpec((1, gw), lambda b, g, t: (0, g))],
        out_specs=pl.BlockSpec((n, gw), lambda b, g, t: (rows(b, g, t), g)),
        out_shape=jax.ShapeDtypeStruct((bsz * t_len, inner), BF16),
        scratch_shapes=[pltpu.VMEM((SUBLANES + n, gw), F32), pltpu.VMEM((SUBLANES + n, n_state), F32),
                        pltpu.VMEM((SUBLANES + n, n_state), F32), pltpu.VMEM((n_state, gw), F32)],
        compiler_params=_params(("parallel", "parallel", "arbitrary")),
        name="mamba2_ssd",
    )(p_zx, p_zx, p_small, p_small, dt_g, acs_g, acs_gt,
      cw_x, cw_b, cw_c, cb_x, cb_b, cb_c, d_exp, norm_w.reshape(1, inner).astype(F32))


def _gated(acc, gate):
    return _sigmoid(gate) * acc


def _gated_add(acc, gate, other):
    return other + _sigmoid(gate) * acc


def _residual(acc, res):
    return res + acc


def _relu_sq(acc):
    return jnp.square(jnp.maximum(acc, 0.0))


def kernel(x, norm_mix_w, w_in, rwkv_mu, rwkv_w0, rwkv_w_up, rwkv_a0, rwkv_a_up, rwkv_g_up, rwkv_k_k, rwkv_k_a, rwkv_r_k, rwkv_ln_w, rwkv_ln_b, ssm_conv_w, ssm_conv_b, ssm_dt_bias, ssm_a_log, ssm_d, ssm_norm_w, w_br_rwkv, w_br_ssm, w_out, norm_mlp_w, w_mlp_up, w_mlp_down, norm_final_w):
    bsz, t_len, d = x.shape
    m = bsz * t_len
    depth = w_in.shape[0]
    width = rwkv_w0.shape[-1]
    wl, al, gl = rwkv_w_up.shape[1], rwkv_a_up.shape[1], rwkv_g_up.shape[1]
    wp, ap, gp = (_round_up(n, LANES) for n in (wl, al, gl))
    rwkv_cols = 3 * width + wl + al + gl
    inner = ssm_norm_w.shape[-1]
    heads = ssm_a_log.shape[-1]
    hp = _round_up(heads, LANES)
    gn = SSM_GROUPS * SSM_STATE

    h = x.reshape(m, d).astype(F32)
    for layer in range(depth):
        o_z = rwkv_cols
        o_wd = 3 * width
        o_ad = o_wd + wl
        o_gd = o_ad + al
        w_t = jnp.transpose(w_in[layer])
        w_rkv = _transposed_cast(w_t, 0, 3 * width)
        n_tail = w_t.shape[0] - o_z
        n_main = n_tail // LANES * LANES
        tail = _transposed_cast(w_t, o_z, n_main)
        tail_end = jnp.transpose(lax.optimization_barrier(w_t[o_z + n_main:])).astype(BF16)
        lora_t = lax.optimization_barrier(w_t[o_wd:o_z])
        lora = lambda lo, hi, padded: _pad_to(jnp.transpose(lora_t[lo - o_wd:hi - o_wd]).astype(BF16), 1, padded)

        def tail_cols(lo, hi):
            if hi <= n_main:
                return tail[:, lo:hi]
            return jnp.concatenate([tail[:, lo:n_main], tail_end[:, :hi - n_main]], axis=1)

        t_bc = 2 * inner
        t_dt = t_bc + 2 * gn
        t_ga = t_dt + heads
        assert t_bc <= n_main
        w_gate = tail_cols(t_ga, t_ga + 2 * d)
        small = [tail_cols(t_bc, t_dt),
                 lora(o_gd, o_gd + gl, gp),
                 _pad_to(tail_cols(t_dt, t_ga), 1, hp),
                 lora(o_wd, o_wd + wl, wp),
                 lora(o_ad, o_ad + al, ap)]
        off_g = 2 * gn
        off_dt = off_g + gp
        off_w = off_dt + hp
        off_a = off_w + wp
        n_small = _round_up(off_a + ap, 1024)
        w_small = _pad_to(jnp.concatenate(small, axis=1), 1, n_small)

        u = _rmsnorm(h, norm_mix_w[layer], BF16)
        p_rkv = _matmul(u, w_rkv, F32, name="proj_rkv")
        p_zx = _matmul(u, tail, F32, n_cols=2 * inner, name="proj_zx")
        p_gate = _matmul(u, w_gate, F32, name="proj_gate")
        p_small = _matmul(u, w_small, F32, name="proj_small")

        y_a = _rwkv_branch(p_rkv, p_small, (off_g, off_w, off_a), bsz, t_len,
                           rwkv_mu[layer], rwkv_w0[layer], rwkv_w_up[layer], rwkv_a0[layer],
                           rwkv_a_up[layer], rwkv_g_up[layer], rwkv_k_k[layer], rwkv_k_a[layer],
                           rwkv_r_k[layer], rwkv_ln_w[layer], rwkv_ln_b[layer])

        dt_sp, acs = _dt_tables(p_small, off_dt, hp, ssm_dt_bias[layer], ssm_a_log[layer])
        y_b = _ssd_branch(p_zx, p_small, (0, gn), dt_sp, acs, bsz, t_len,
                          ssm_conv_w[layer], ssm_conv_b[layer], ssm_d[layer], ssm_norm_w[layer])

        part = _matmul(y_a, w_br_rwkv[layer], F32, epilogue=_gated,
                       extras=((p_gate, 0),), name="branch_rwkv")
        merged = _matmul(y_b, w_br_ssm[layer].astype(BF16), BF16, epilogue=_gated_add,
                         extras=((p_gate, d), (part, 0)), name="branch_ssm")
        h = _matmul(merged, w_out[layer], F32, epilogue=_residual,
                    extras=((h, 0),), name="out_proj")

        u = _rmsnorm(h, norm_mlp_w[layer], BF16)
        hidden = _matmul(u, w_mlp_up[layer], BF16, epilogue=_relu_sq,
                         name="mlp_up")
        h = _matmul(hidden, w_mlp_down[layer].astype(BF16), F32, epilogue=_residual,
                    extras=((h, 0),), name="mlp_down")
    return _rmsnorm(h, norm_final_w, x.dtype).reshape(bsz, t_len, d)
```
